```python
import math
import jax, jax.numpy as jnp
from jax import lax
import numpy as np

D_MODEL = 1024
BATCH = 8
SEQ = 2048
DEPTH = 2
DEC_BATCH = 128
DEC_SEQ = 4
PAST_LEN = 16384
PAGE_SIZE = 128

N_MIXERS = 2
HEAD_DIM = 64
WINDOW = 128
BLOCK = WINDOW
A_HEADS = D_MODEL // HEAD_DIM
A_KV_HEADS = A_HEADS // 4
A_GROUP = A_HEADS // A_KV_HEADS
B_HEADS = D_MODEL // (2 * HEAD_DIM)
B_KV_HEADS = B_HEADS // 2
B_GROUP = B_HEADS // B_KV_HEADS
B_DK = 2 * HEAD_DIM
D_FF = ((8 * D_MODEL + 3 * 256 - 1) // (3 * 256)) * 256
N_A_LAYERS = (DEPTH + 1) // 2
N_B_LAYERS = DEPTH // 2
RMS_EPS = 1e-6

kernel_name = "hybrid_swa_sink_diffattn_alibi_adaln_step"


def rms_norm(x, g):
    xf = x.astype(jnp.float32)
    y = xf * lax.rsqrt(jnp.mean(xf * xf, axis=-1, keepdims=True) + RMS_EPS)
    return (y * g.astype(jnp.float32)).astype(x.dtype)


def alibi_slopes(n):
    return jnp.exp2(-8.0 * jnp.arange(1, n + 1, dtype=jnp.float32) / n)


def adaln(c, w, b):
    mod = jax.nn.silu(c) @ w + b
    return [m[:, None, :] for m in jnp.split(mod, 6, axis=-1)]


def swiglu(h, w_gu, w_down):
    g, u = jnp.split(h @ w_gu, 2, axis=-1)
    return (jax.nn.silu(g) * u) @ w_down


def sink_window_attention(q, k, v, rel, valid, sinks):
    slopes = alibi_slopes(A_HEADS).reshape(A_KV_HEADS, A_GROUP)
    s = jnp.einsum('...qkgd,...skd->...kgqs', q.astype(jnp.float32), k.astype(jnp.float32)) * HEAD_DIM ** -0.5
    s = s - slopes[:, :, None, None] * rel
    s = jnp.where(valid, s, -jnp.inf)
    sink = sinks.astype(jnp.float32).reshape(A_KV_HEADS, A_GROUP)[:, :, None]
    m = jnp.maximum(s.max(-1), sink)
    p = jnp.exp(s - m[..., None])
    p = p / (p.sum(-1) + jnp.exp(sink - m))[..., None]
    o = jnp.einsum('...kgqs,...skd->...qkgd', p, v.astype(jnp.float32))
    return o.astype(q.dtype)


def split_qkv_a(qkv):
    lead = qkv.shape[:-1]
    nq, nk = A_HEADS * HEAD_DIM, A_KV_HEADS * HEAD_DIM
    q = qkv[..., :nq].reshape(*lead, A_KV_HEADS, A_GROUP, HEAD_DIM)
    k = qkv[..., nq:nq + nk].reshape(*lead, A_KV_HEADS, HEAD_DIM)
    v = qkv[..., nq + nk:].reshape(*lead, A_KV_HEADS, HEAD_DIM)
    return q, k, v


def window_mixer_prompt(h, w_qkv, w_o, sinks):
    B, S, _ = h.shape
    q, k, v = split_qkv_a(h @ w_qkv)
    nb = S // BLOCK
    qb = q.reshape(B, nb, BLOCK, A_KV_HEADS, A_GROUP, HEAD_DIM)
    kb = k.reshape(B, nb, BLOCK, A_KV_HEADS, HEAD_DIM)
    vb = v.reshape(B, nb, BLOCK, A_KV_HEADS, HEAD_DIM)
    pad = jnp.zeros_like(kb[:, :1])
    kk = jnp.concatenate([jnp.concatenate([pad, kb[:, :-1]], axis=1), kb], axis=2)
    vv = jnp.concatenate([jnp.concatenate([pad, vb[:, :-1]], axis=1), vb], axis=2)
    blk = jnp.arange(nb)[:, None] * BLOCK
    qpos = blk + jnp.arange(BLOCK)[None, :]
    kpos = blk - BLOCK + jnp.arange(2 * BLOCK)[None, :]
    rel = qpos[:, :, None] - kpos[:, None, :]
    valid = (rel >= 0) & (rel < WINDOW) & (kpos[:, None, :] >= 0)
    o = sink_window_attention(qb, kk, vv, rel.astype(jnp.float32)[:, None, None], valid[:, None, None], sinks)
    y = o.reshape(B, S, A_HEADS * HEAD_DIM) @ w_o
    return y, k[:, -WINDOW:], v[:, -WINDOW:]


def window_mixer_sample(h, buf_k, buf_v, w_qkv, w_o, sinks):
    DB, T, _ = h.shape
    q, k, v = split_qkv_a(h @ w_qkv)
    kk = jnp.concatenate([buf_k, k.astype(buf_k.dtype)], axis=1)
    vv = jnp.concatenate([buf_v, v.astype(buf_v.dtype)], axis=1)
    kpos = jnp.arange(WINDOW + T) - WINDOW
    qpos = jnp.arange(T)
    rel = qpos[:, None] - kpos[None, :]
    valid = (rel >= 0) & (rel < WINDOW)
    o = sink_window_attention(q, kk, vv, rel.astype(jnp.float32), valid, sinks)
    y = o.reshape(DB, T, A_HEADS * HEAD_DIM) @ w_o
    return y, kk[:, -WINDOW:], vv[:, -WINDOW:]


def diff_lambda(lam_params, lam_init):
    lp = lam_params.astype(jnp.float32)
    return jnp.exp(jnp.sum(lp[0] * lp[1])) - jnp.exp(jnp.sum(lp[2] * lp[3])) + lam_init


def diff_attention(q, k, v, rel, valid, lam, lam_init, subln_g):
    slopes = alibi_slopes(B_HEADS).reshape(B_KV_HEADS, B_GROUP)
    s = jnp.einsum('...qkgcd,...skcd->...ckgqs', q.astype(jnp.float32), k.astype(jnp.float32)) * HEAD_DIM ** -0.5
    s = s - slopes[:, :, None, None] * rel
    s = jnp.where(valid, s, -jnp.inf)
    p = jax.nn.softmax(s, axis=-1)
    a = p[..., 0, :, :, :, :] - lam * p[..., 1, :, :, :, :]
    o = jnp.einsum('...kgqs,...skd->...qkgd', a, v.astype(jnp.float32))
    o = rms_norm(o, subln_g) * (1.0 - lam_init)
    return o.astype(q.dtype)


def split_qkv_b(qkv):
    lead = qkv.shape[:-1]
    nq, nk = B_HEADS * B_DK, B_KV_HEADS * B_DK
    q = qkv[..., :nq].reshape(*lead, B_KV_HEADS, B_GROUP, 2, HEAD_DIM)
    k = qkv[..., nq:nq + nk].reshape(*lead, B_KV_HEADS, B_DK)
    v = qkv[..., nq + nk:].reshape(*lead, B_KV_HEADS, B_DK)
    return q, k, v


def diff_mixer_prompt(h, w_qkv, w_o, lam, lam_init, subln_g):
    B, S, _ = h.shape
    q, k, v = split_qkv_b(h @ w_qkv)
    nb = S // BLOCK
    q_blocks = jnp.moveaxis(q.reshape(B, nb, BLOCK, B_KV_HEADS, B_GROUP, 2, HEAD_DIM), 1, 0)
    k4 = k.reshape(B, S, B_KV_HEADS, 2, HEAD_DIM)
    kpos = jnp.arange(S)

    def one_block(args):
        q_blk, i = args
        qpos = i * BLOCK + jnp.arange(BLOCK)
        rel = qpos[:, None] - kpos[None, :]
        return diff_attention(q_blk, k4, v, rel.astype(jnp.float32), rel >= 0, lam, lam_init, subln_g)

    o = lax.map(one_block, (q_blocks, jnp.arange(nb)))
    o = jnp.moveaxis(o, 0, 1).reshape(B, S, B_HEADS * B_DK)
    return o @ w_o, k, v


def diff_mixer_sample(h, cache_k, cache_v, layer_b, page_table, w_qkv, w_o, lam, lam_init, subln_g):
    DB, T, _ = h.shape
    q, k, v = split_qkv_b(h @ w_qkv)
    past = page_table.shape[1] * PAGE_SIZE
    kpos = jnp.arange(past + T)
    qpos = past + jnp.arange(T)
    rel = qpos[:, None] - kpos[None, :]
    relf, valid = rel.astype(jnp.float32), rel >= 0

    def one_seq(args):
        q_s, k_s, v_s, pages = args
        k_past = cache_k[layer_b, pages].reshape(past, B_KV_HEADS, B_DK)
        v_past = cache_v[layer_b, pages].reshape(past, B_KV_HEADS, B_DK)
        k_all = jnp.concatenate([k_past, k_s.astype(k_past.dtype)], axis=0)
        v_all = jnp.concatenate([v_past, v_s.astype(v_past.dtype)], axis=0)
        return diff_attention(q_s, k_all.reshape(past + T, B_KV_HEADS, 2, HEAD_DIM), v_all,
                              relf, valid, lam, lam_init, subln_g)

    o = lax.map(one_seq, (q, k, v, page_table))
    return o.reshape(DB, T, B_HEADS * B_DK) @ w_o, k, v


def pre_mod(x, g, shift, scale):
    return rms_norm(x, g) * (1.0 + scale) + shift


def post_add(x, y, g, gate):
    return x + gate * rms_norm(y, g)


def ffn_block(x, mod, g_pre, g_post, w_gu, w_down):
    h = pre_mod(x, g_pre, mod[3], mod[4])
    return post_add(x, swiglu(h, w_gu, w_down), g_post, mod[5])


def setup_inputs(seed: int = 0) -> dict:
    key = jax.random.key(seed)
    ks = jax.random.split(key, 24)
    n_pages = PAST_LEN // PAGE_SIZE
    n_used = DEC_BATCH * n_pages
    n_pool = n_used + max(1, n_used // 4)

    def nrm(k, shape, scale=1.0):
        return scale * jax.random.normal(k, shape, jnp.float32)

    qkv_a = (A_HEADS + 2 * A_KV_HEADS) * HEAD_DIM
    qkv_b = (B_HEADS + 2 * B_KV_HEADS) * B_DK
    return {
        "x_prompt": nrm(ks[0], (BATCH, SEQ, D_MODEL)),
        "x_sample": nrm(ks[1], (DEC_BATCH, DEC_SEQ, D_MODEL)),
        "cache_win_k": nrm(ks[2], (N_A_LAYERS, DEC_BATCH, WINDOW, A_KV_HEADS, HEAD_DIM)),
        "cache_win_v": nrm(ks[3], (N_A_LAYERS, DEC_BATCH, WINDOW, A_KV_HEADS, HEAD_DIM)),
        "cache_diff_k": nrm(ks[4], (N_B_LAYERS, n_pool, PAGE_SIZE, B_KV_HEADS, B_DK)),
        "cache_diff_v": nrm(ks[5], (N_B_LAYERS, n_pool, PAGE_SIZE, B_KV_HEADS, B_DK)),
        "page_table": jax.random.permutation(ks[6], n_pool)[:n_used].reshape(DEC_BATCH, n_pages).astype(jnp.int32),
        "c_prompt": nrm(ks[7], (BATCH, D_MODEL)),
        "c_sample": nrm(ks[8], (DEC_BATCH, D_MODEL)),
        "ada_w": nrm(ks[9], (DEPTH, D_MODEL, 6 * D_MODEL), 0.5 * D_MODEL ** -0.5),
        "ada_b": nrm(ks[10], (DEPTH, 6 * D_MODEL), 0.02),
        "norm_g": 1.0 + nrm(ks[11], (DEPTH, 4, D_MODEL), 0.05),
        "w_qkv_a": nrm(ks[12], (N_A_LAYERS, D_MODEL, qkv_a), D_MODEL ** -0.5),
        "w_o_a": nrm(ks[13], (N_A_LAYERS, A_HEADS * HEAD_DIM, D_MODEL), (A_HEADS * HEAD_DIM) ** -0.5),
        "sinks_a": nrm(ks[14], (N_A_LAYERS, A_HEADS), 0.5),
        "w_qkv_b": nrm(ks[15], (N_B_LAYERS, D_MODEL, qkv_b), D_MODEL ** -0.5),
        "w_o_b": nrm(ks[16], (N_B_LAYERS, B_HEADS * B_DK, D_MODEL), (B_HEADS * B_DK) ** -0.5),
        "lambda_b": nrm(ks[17], (N_B_LAYERS, 4, HEAD_DIM), 0.1),
        "subln_b": 1.0 + nrm(ks[18], (N_B_LAYERS, B_DK), 0.05),
        "w_gu": nrm(ks[19], (DEPTH, D_MODEL, 2 * D_FF), D_MODEL ** -0.5),
        "w_down": nrm(ks[20], (DEPTH, D_FF, D_MODEL), D_FF ** -0.5),
    }


def reference(x_prompt, x_sample, cache_win_k, cache_win_v, cache_diff_k, cache_diff_v, page_table,
              c_prompt, c_sample, ada_w, ada_b, norm_g, w_qkv_a, w_o_a, sinks_a,
              w_qkv_b, w_o_b, lambda_b, subln_b, w_gu, w_down):
    xp, xs = x_prompt, x_sample
    wkp, wvp, wks, wvs = [], [], [], []
    dkp, dvp, dks, dvs = [], [], [], []
    ia = ib = 0
    for layer in range(DEPTH):
        mp = adaln(c_prompt, ada_w[layer], ada_b[layer])
        ms = adaln(c_sample, ada_w[layer], ada_b[layer])
        hp = pre_mod(xp, norm_g[layer, 0], mp[0], mp[1])
        hs = pre_mod(xs, norm_g[layer, 0], ms[0], ms[1])
        if layer % N_MIXERS == 0:
            yp, kp, vp = window_mixer_prompt(hp, w_qkv_a[ia], w_o_a[ia], sinks_a[ia])
            ys, ks_, vs_ = window_mixer_sample(hs, cache_win_k[ia], cache_win_v[ia],
                                               w_qkv_a[ia], w_o_a[ia], sinks_a[ia])
            wkp.append(kp); wvp.append(vp); wks.append(ks_); wvs.append(vs_)
            ia += 1
        else:
            lam_init = 0.8 - 0.6 * math.exp(-0.3 * layer)
            lam = diff_lambda(lambda_b[ib], lam_init)
            yp, kp, vp = diff_mixer_prompt(hp, w_qkv_b[ib], w_o_b[ib], lam, lam_init, subln_b[ib])
            ys, ks_, vs_ = diff_mixer_sample(hs, cache_diff_k, cache_diff_v, ib, page_table,
                                             w_qkv_b[ib], w_o_b[ib], lam, lam_init, subln_b[ib])
            dkp.append(kp); dvp.append(vp); dks.append(ks_); dvs.append(vs_)
            ib += 1
        xp = post_add(xp, yp, norm_g[layer, 1], mp[2])
        xs = post_add(xs, ys, norm_g[layer, 1], ms[2])
        xp = ffn_block(xp, mp, norm_g[layer, 2], norm_g[layer, 3], w_gu[layer], w_down[layer])
        xs = ffn_block(xs, ms, norm_g[layer, 2], norm_g[layer, 3], w_gu[layer], w_down[layer])
    win_k_prompt, win_v_prompt = jnp.stack(wkp), jnp.stack(wvp)
    win_k_sample, win_v_sample = jnp.stack(wks), jnp.stack(wvs)
    diff_k_prompt, diff_v_prompt = jnp.stack(dkp), jnp.stack(dvp)
    diff_k_sample, diff_v_sample = jnp.stack(dks), jnp.stack(dvs)
    return (xp, xs, win_k_prompt, win_v_prompt, win_k_sample, win_v_sample,
            diff_k_prompt, diff_v_prompt, diff_k_sample, diff_v_sample)
```

```python
import functools
import math

import jax
import jax.numpy as jnp
from jax import lax
from jax.experimental import pallas as pl
from jax.experimental.pallas import tpu as pltpu

F32 = jnp.float32
BF16 = jnp.bfloat16

D_MODEL = 1024
HEAD_DIM = 64
WINDOW = 128
A_HEADS = 16
A_KV_HEADS = 4
A_GROUP = 4
B_HEADS = 8
B_KV_HEADS = 4
B_GROUP = 2
B_DK = 128
D_FF = 2816
PAGE_SIZE = 128
RMS_EPS = 1e-6
N_MOD = 6

LANES = 128
MXU_WIDTH = 256
VMEM_LIMIT = 56 * 1024 * 1024
NEG = -1e30
BIG = 1e30

ROW_TILE = 512
SAMPLE_TILE = 128
FF_CHUNK = MXU_WIDTH
ATT_T = 256
PAGES_PER_STEP = 8
WIN_SAMPLE_SEQS = 8


def _dot(a, b):
    return jnp.dot(a, b, preferred_element_type=F32)


def _dot_nt(a, b):
    return lax.dot_general(a, b, (((1,), (1,)), ((), ())), preferred_element_type=F32)


def _rms(x, g):
    ms = jnp.mean(x * x, axis=-1, keepdims=True)
    return x * lax.rsqrt(ms + RMS_EPS) * g


def _silu(x):
    return x / (1.0 + jnp.exp(-x))


def _params(*sem):
    return pltpu.CompilerParams(dimension_semantics=sem, vmem_limit_bytes=VMEM_LIMIT)


def _adaln_kernel(c_ref, w_ref, b_ref, o_ref):
    s = _silu(c_ref[...]).astype(BF16)
    o_ref[0] = _dot(s, w_ref[0].astype(BF16)) + b_ref[0]


def _adaln(c_all, ada_w, ada_b):
    depth, d, n = ada_w.shape
    rows = c_all.shape[0]
    tn = D_MODEL
    return pl.pallas_call(
        _adaln_kernel,
        out_shape=jax.ShapeDtypeStruct((depth, rows, n), F32),
        grid=(depth, n // tn),
        in_specs=[
            pl.BlockSpec((rows, d), lambda l, t: (0, 0)),
            pl.BlockSpec((1, d, tn), lambda l, t: (l, 0, t)),
            pl.BlockSpec((1, 1, tn), lambda l, t: (l, 0, t)),
        ],
        out_specs=pl.BlockSpec((1, rows, tn), lambda l, t: (l, 0, t)),
        compiler_params=_params("parallel", "parallel"),
        name="adaln",
    )(c_all, ada_w, ada_b.reshape(depth, 1, n))


def _qkv_kernel(x_ref, mod_ref, g_ref, w_ref, q_ref, k_ref, v_ref, kb_ref, vb_ref, *, nq, nk):
    shift = mod_ref[0, :, 0:D_MODEL]
    scale = mod_ref[0, :, D_MODEL:2 * D_MODEL]
    h = _rms(x_ref[...], g_ref[0:1, :]) * (1.0 + scale) + shift
    qkv = _dot(h.astype(BF16), w_ref[...])
    q_ref[...] = (qkv[:, :nq] * HEAD_DIM ** -0.5).astype(BF16)
    k = qkv[:, nq:nq + nk]
    v = qkv[:, nq + nk:]
    k_ref[...] = k
    v_ref[...] = v
    kb_ref[...] = k.astype(BF16)
    vb_ref[...] = v.astype(BF16)


def _qkv(x, mod, g, w, nq, nk, tm, tiles_per_group):
    rows = x.shape[0]
    r = mod.shape[1]
    row_spec = lambda n: pl.BlockSpec((tm, n), lambda i: (i, 0))
    return pl.pallas_call(
        functools.partial(_qkv_kernel, nq=nq, nk=nk),
        out_shape=(
            jax.ShapeDtypeStruct((rows, nq), BF16),
            jax.ShapeDtypeStruct((rows, nk), F32),
            jax.ShapeDtypeStruct((rows, nk), F32),
            jax.ShapeDtypeStruct((rows, nk), BF16),
            jax.ShapeDtypeStruct((rows, nk), BF16),
        ),
        grid=(rows // tm,),
        in_specs=[
            row_spec(D_MODEL),
            pl.BlockSpec((1, r, N_MOD * D_MODEL), lambda i: (i // tiles_per_group, 0, 0)),
            pl.BlockSpec(g.shape, lambda i: (0, 0)),
            pl.BlockSpec(w.shape, lambda i: (0, 0)),
        ],
        out_specs=(row_spec(nq), row_spec(nk), row_spec(nk), row_spec(nk), row_spec(nk)),
        compiler_params=_params("parallel"),
        name="prenorm_qkv",
    )(x, mod, g, w)


def _out_ffn_kernel(o_ref, x_ref, mod_ref, g_ref, wo_ref, wgu_ref, wd_ref, y_ref):
    gate1 = mod_ref[0, :, 2 * D_MODEL:3 * D_MODEL]
    shift2 = mod_ref[0, :, 3 * D_MODEL:4 * D_MODEL]
    scale2 = mod_ref[0, :, 4 * D_MODEL:5 * D_MODEL]
    gate2 = mod_ref[0, :, 5 * D_MODEL:6 * D_MODEL]
    y = _dot(o_ref[...], wo_ref[...])
    x1 = x_ref[...] + gate1 * _rms(y, g_ref[1:2, :])
    h = (_rms(x1, g_ref[2:3, :]) * (1.0 + scale2) + shift2).astype(BF16)
    acc = jnp.zeros(x1.shape, F32)
    for c in range(D_FF // FF_CHUNK):
        lo = c * FF_CHUNK
        gch = _dot(h, wgu_ref[:, lo:lo + FF_CHUNK])
        uch = _dot(h, wgu_ref[:, D_FF + lo:D_FF + lo + FF_CHUNK])
        a = (_silu(gch) * uch).astype(BF16)
        acc = acc + _dot(a, wd_ref[lo:lo + FF_CHUNK, :])
    y_ref[...] = x1 + gate2 * _rms(acc, g_ref[3:4, :])


def _out_ffn(o, x, mod, g, wo, wgu, wd, tm, tiles_per_group):
    rows = x.shape[0]
    r = mod.shape[1]
    row_spec = pl.BlockSpec((tm, D_MODEL), lambda i: (i, 0))
    const = lambda a: pl.BlockSpec(a.shape, lambda i: (0, 0), pipeline_mode=pl.Buffered(1))
    return pl.pallas_call(
        _out_ffn_kernel,
        out_shape=jax.ShapeDtypeStruct((rows, D_MODEL), F32),
        grid=(rows // tm,),
        in_specs=[
            row_spec,
            row_spec,
            pl.BlockSpec((1, r, N_MOD * D_MODEL), lambda i: (i // tiles_per_group, 0, 0)),
            const(g), const(wo), const(wgu), const(wd),
        ],
        out_specs=row_spec,
        compiler_params=_params("parallel"),
        name="out_ffn",
    )(o, x, mod, g, wo, wgu, wd)


def _a_slope(h):
    return 2.0 ** (-8.0 * (h + 1) / A_HEADS)


def _b_slope(h):
    return 2.0 ** (-8.0 * (h + 1) / B_HEADS)


def _half_lane_pair(x, kv_head, lo):
    rolled = pltpu.roll(x, HEAD_DIM, 1)
    first, second = (x, rolled) if kv_head % 2 == 0 else (rolled, x)
    return (jnp.where(lo, first, 0.0).astype(BF16), jnp.where(lo, 0.0, second).astype(BF16))


def _win_prompt_kernel(sink_ref, q_ref, kc_ref, kp_ref, vc_ref, vp_ref, o_ref):
    i = pl.program_id(1)
    kk = jnp.concatenate([kp_ref[...], kc_ref[...]], axis=0)
    vv = jnp.concatenate([vp_ref[...], vc_ref[...]], axis=0)
    row = lax.broadcasted_iota(jnp.int32, (WINDOW, 2 * WINDOW), 0)
    col = lax.broadcasted_iota(jnp.int32, (WINDOW, 2 * WINDOW), 1)
    rel = row - col + WINDOW
    valid = (rel >= 0) & (rel < WINDOW) & ((col >= WINDOW) | (i > 0))
    relm = jnp.where(valid, rel.astype(F32), BIG)
    lo = lax.broadcasted_iota(jnp.int32, (2 * WINDOW, LANES), 1) < HEAD_DIM
    for j in range(A_KV_HEADS):
        sl = slice((j // 2) * LANES, (j // 2 + 1) * LANES)
        k_pair = _half_lane_pair(kk[:, sl], j, lo)
        v_pair = _half_lane_pair(vv[:, sl], j, lo)
        for p in range(A_GROUP // 2):
            qsl = slice((2 * j + p) * LANES, (2 * j + p + 1) * LANES)
            qs = q_ref[:, qsl]
            out = None
            for half in range(2):
                h = A_GROUP * j + 2 * p + half
                s = _dot_nt(qs, k_pair[half]) - _a_slope(h) * relm
                sink = sink_ref[h]
                m = jnp.maximum(jnp.max(s, axis=-1, keepdims=True), sink)
                pe = jnp.exp(s - m)
                l = jnp.sum(pe, axis=-1, keepdims=True) + jnp.exp(sink - m)
                o = _dot(pe.astype(BF16), v_pair[half]) / l
                out = o if out is None else out + o
            o_ref[:, qsl] = out.astype(BF16)


def _win_prompt(q, k, v, sinks, batch, seq):
    nb = seq // WINDOW
    nk = A_KV_HEADS * HEAD_DIM
    cur = lambda b, i: (b * nb + i, 0)
    prev = lambda b, i: (b * nb + jnp.maximum(i - 1, 0), 0)
    smem = pl.BlockSpec(memory_space=pltpu.SMEM)
    return pl.pallas_call(
        _win_prompt_kernel,
        out_shape=jax.ShapeDtypeStruct(q.shape, BF16),
        grid=(batch, nb),
        in_specs=[
            smem,
            pl.BlockSpec((WINDOW, D_MODEL), cur),
            pl.BlockSpec((WINDOW, nk), cur),
            pl.BlockSpec((WINDOW, nk), prev),
            pl.BlockSpec((WINDOW, nk), cur),
            pl.BlockSpec((WINDOW, nk), prev),
        ],
        out_specs=pl.BlockSpec((WINDOW, D_MODEL), cur),
        compiler_params=_params("parallel", "parallel"),
        name="win_prompt",
    )(sinks, q, k, k, v, v)


def _win_sample_kernel(q_ref, kbuf_ref, knew_ref, vbuf_ref, vnew_ref, sink_ref, slope_ref, o_ref):
    rows = A_HEADS * 4
    keys = 2 * WINDOW
    r = lax.broadcasted_iota(jnp.int32, (rows, keys), 0)
    pos = lax.broadcasted_iota(jnp.int32, (rows, keys), 1)
    rel = (r % 4) + WINDOW - pos
    valid = (rel >= 0) & (rel < WINDOW)
    bias = jnp.where(valid, -slope_ref[...] * rel.astype(F32), NEG)
    nk = A_KV_HEADS * HEAD_DIM
    own = (lax.broadcasted_iota(jnp.int32, (rows, nk), 1) // HEAD_DIM
           == lax.broadcasted_iota(jnp.int32, (rows, nk), 0) // (A_GROUP * 4))
    sink = sink_ref[...]
    pad = jnp.zeros((keys - WINDOW - 8, A_KV_HEADS * HEAD_DIM), F32)
    for b in range(WIN_SAMPLE_SEQS):
        kall = jnp.concatenate([kbuf_ref[b], knew_ref[b], pad], axis=0).astype(BF16)
        vall = jnp.concatenate([vbuf_ref[b], vnew_ref[b], pad], axis=0).astype(BF16)
        s = _dot_nt(q_ref[b], kall) + bias
        m = jnp.maximum(jnp.max(s, axis=-1, keepdims=True), sink)
        pe = jnp.exp(s - m)
        l = jnp.sum(pe, axis=-1, keepdims=True) + jnp.exp(sink - m)
        o = jnp.where(own, _dot(pe.astype(BF16), vall) / l, 0.0)
        o_ref[b] = o[0:16] + o[16:32] + o[32:48] + o[48:64]


def _win_sample(qrows, kbuf, knew, vbuf, vnew, sink_rows, slope_rows):
    nseq = qrows.shape[0]
    sb = WIN_SAMPLE_SEQS
    nk = A_KV_HEADS * HEAD_DIM
    blk = lambda r, c: pl.BlockSpec((sb, r, c), lambda i: (i, 0, 0))
    col = pl.BlockSpec((A_HEADS * 4, 1), lambda i: (0, 0))
    return pl.pallas_call(
        _win_sample_kernel,
        out_shape=jax.ShapeDtypeStruct((nseq, 16, nk), F32),
        grid=(nseq // sb,),
        in_specs=[blk(A_HEADS * 4, nk), blk(WINDOW, nk), blk(8, nk), blk(WINDOW, nk), blk(8, nk), col, col],
        out_specs=blk(16, nk),
        compiler_params=_params("parallel"),
        name="win_sample",
    )(qrows, kbuf, knew, vbuf, vnew, sink_rows, slope_rows)


def _lambda(lam_ref, lam_init):
    lp = lam_ref[...]
    e1 = jnp.exp(jnp.sum(lp[0:1] * lp[1:2], axis=-1, keepdims=True))
    e2 = jnp.exp(jnp.sum(lp[2:3] * lp[3:4], axis=-1, keepdims=True))
    return e1 - e2 + lam_init


def _online_update(s, off, v_dot, m_ref, l_ref, acc_ref):
    m_old = m_ref[...]
    m_new = jnp.maximum(m_old, jnp.max(s, axis=-1, keepdims=True) + off)
    pe = jnp.exp(s - (m_new - off))
    alpha = jnp.exp(m_old - m_new)
    l_ref[...] = alpha * l_ref[...] + jnp.sum(pe, axis=-1, keepdims=True)
    acc_ref[...] = alpha * acc_ref[...] + v_dot(pe.astype(BF16))
    m_ref[...] = m_new


def _diff_prompt_kernel(slope_ref, q_ref, k_ref, v_ref, lam_ref, sg_ref, o_ref,
                        qs_ref, m_ref, l_ref, acc_ref, bias_ref, *, lam_init):
    t = ATT_T
    j = pl.program_id(1)
    i = pl.program_id(2)
    lo = lax.broadcasted_iota(jnp.int32, (t, LANES), 1) < HEAD_DIM
    row = lax.broadcasted_iota(jnp.int32, (t, t), 0)
    col = lax.broadcasted_iota(jnp.int32, (t, t), 1)
    dist = (col - row).astype(F32)
    zero = jnp.zeros((t, LANES), BF16)
    for g in range(B_GROUP):
        qg = q_ref[:, g * LANES:(g + 1) * LANES]
        qs_ref[g * t:(g + 1) * t, :] = jnp.where(lo, qg, zero)
        qs_ref[(B_GROUP + g) * t:(B_GROUP + g + 1) * t, :] = jnp.where(lo, zero, qg)
        bg = slope_ref[B_GROUP * j + g] * dist
        bias_ref[g * t:(g + 1) * t, :] = bg
        bias_ref[(B_GROUP + g) * t:(B_GROUP + g + 1) * t, :] = bg
    m_ref[...] = jnp.full(m_ref.shape, NEG, F32)
    l_ref[...] = jnp.zeros(l_ref.shape, F32)
    acc_ref[...] = jnp.zeros(acc_ref.shape, F32)
    rg = (lax.broadcasted_iota(jnp.int32, (2 * B_GROUP * t, 1), 0) // t) % B_GROUP
    slope_rows = jnp.where(rg == 0, slope_ref[B_GROUP * j], slope_ref[B_GROUP * j + 1])

    def chunk(c, masked):
        start = pl.multiple_of(c * t, t)
        kc = k_ref[pl.ds(start, t), :]
        vc = v_ref[pl.ds(start, t), :]
        s = _dot_nt(qs_ref[...], kc) + bias_ref[...]
        if masked:
            r4 = lax.broadcasted_iota(jnp.int32, s.shape, 0) % t
            c4 = lax.broadcasted_iota(jnp.int32, s.shape, 1)
            s = jnp.where(c4 <= r4, s, NEG)
        off = slope_rows * ((c - i) * t).astype(F32)
        _online_update(s, off, lambda p: _dot(p, vc), m_ref, l_ref, acc_ref)

    def body(c, carry):
        chunk(c, False)
        return carry

    lax.fori_loop(0, i, body, 0)
    chunk(i, True)

    lam = _lambda(lam_ref, lam_init)
    a = acc_ref[...] / l_ref[...]
    for g in range(B_GROUP):
        o = a[g * t:(g + 1) * t] - lam * a[(B_GROUP + g) * t:(B_GROUP + g + 1) * t]
        o = _rms(o, sg_ref[...]) * (1.0 - lam_init)
        o_ref[:, g * LANES:(g + 1) * LANES] = o.astype(BF16)


def _diff_prompt(q, kb, vb, slopes, lam_p, subln, batch, seq, lam_init):
    t = ATT_T
    nq = seq // t
    stack = 2 * B_GROUP * t
    width = B_GROUP * B_DK
    smem = pl.BlockSpec(memory_space=pltpu.SMEM)
    qo_spec = pl.BlockSpec((t, width), lambda b, j, i: (b * nq + i, j))
    kv_spec = pl.BlockSpec((seq, B_DK), lambda b, j, i: (b, j))
    return pl.pallas_call(
        functools.partial(_diff_prompt_kernel, lam_init=lam_init),
        out_shape=jax.ShapeDtypeStruct(q.shape, BF16),
        grid=(batch, B_KV_HEADS, nq),
        in_specs=[
            smem, qo_spec, kv_spec, kv_spec,
            pl.BlockSpec(lam_p.shape, lambda b, j, i: (0, 0)),
            pl.BlockSpec(subln.shape, lambda b, j, i: (0, 0)),
        ],
        out_specs=qo_spec,
        scratch_shapes=[
            pltpu.VMEM((stack, LANES), BF16),
            pltpu.VMEM((stack, 1), F32),
            pltpu.VMEM((stack, 1), F32),
            pltpu.VMEM((stack, B_DK), F32),
            pltpu.VMEM((stack, t), F32),
        ],
        compiler_params=_params("parallel", "parallel", "parallel"),
        name="diff_prompt",
    )(slopes, q, kb, vb, lam_p, subln)


def _diff_sample_kernel(pt_ref, q_ref, kn_ref, vn_ref, *rest, past, lam_init):
    del pt_ref
    npg = PAGES_PER_STEP
    k_refs = rest[:npg]
    v_refs = rest[npg:2 * npg]
    slope_ref, t_ref, lam_ref, sg_ref, o_ref, m_ref, l_ref, acc_ref = rest[2 * npg:]
    step = pl.program_id(1)
    rows = 2 * B_HEADS * 4

    @pl.when(step == 0)
    def _():
        m_ref[...] = jnp.full(m_ref.shape, NEG, F32)
        l_ref[...] = jnp.zeros(l_ref.shape, F32)
        acc_ref[...] = jnp.zeros(acc_ref.shape, F32)

    q = q_ref[...]
    slope = slope_ref[...]
    tq = t_ref[...]
    lane = lax.broadcasted_iota(jnp.int32, (rows, PAGE_SIZE), 1).astype(F32)

    def update(ks, vs, base, mask):
        parts = [_dot_nt(q, kp.astype(BF16)) + slope * (lane + float(p * PAGE_SIZE))
                 for p, kp in enumerate(ks)]
        s = parts[0] if len(parts) == 1 else jnp.concatenate(parts, axis=1)
        if mask is not None:
            s = jnp.where(mask, s, NEG)
        off = slope * (base - float(past) - tq)

        def v_dot(p):
            out = None
            for n, vp in enumerate(vs):
                term = _dot(p[:, n * PAGE_SIZE:(n + 1) * PAGE_SIZE], vp.astype(BF16))
                out = term if out is None else out + term
            return out

        _online_update(s, off, v_dot, m_ref, l_ref, acc_ref)

    base = (step * (npg * PAGE_SIZE)).astype(F32)
    update([r[...] for r in k_refs], [r[...] for r in v_refs], base, None)

    @pl.when(step == pl.num_programs(1) - 1)
    def _():
        pad = jnp.zeros((PAGE_SIZE - 8, B_KV_HEADS * B_DK), F32)
        knew = jnp.concatenate([kn_ref[...], pad], axis=0)
        vnew = jnp.concatenate([vn_ref[...], pad], axis=0)
        update([knew], [vnew], float(past), lane <= tq)

        lam = _lambda(lam_ref, lam_init)
        a = acc_ref[...] / l_ref[...]
        half = rows // 2
        dm = a[0:half] - lam * a[half:rows]
        rj = lax.broadcasted_iota(jnp.int32, dm.shape, 0) // (B_GROUP * 4)
        lj = lax.broadcasted_iota(jnp.int32, dm.shape, 1) // B_DK
        dm = jnp.where(rj == lj, dm, 0.0)
        r8 = dm[0:8] + dm[8:16] + dm[16:24] + dm[24:32]
        for jj in range(B_KV_HEADS):
            blk = r8[:, jj * B_DK:(jj + 1) * B_DK]
            o_ref[:, jj * B_DK:(jj + 1) * B_DK] = _rms(blk, sg_ref[...]) * (1.0 - lam_init)


def _diff_sample(page_table, qrows, knew, vnew, cache_k, cache_v, slope_rows, t_rows, lam_p, subln, lam_init):
    nseq, n_pages = page_table.shape
    npg = PAGES_PER_STEP
    width = B_KV_HEADS * B_DK
    rows = 2 * B_HEADS * 4
    past = n_pages * PAGE_SIZE

    def page_spec(p):
        return pl.BlockSpec((None, PAGE_SIZE, width), lambda b, s, pt: (pt[b, s * npg + p], 0, 0))

    per_seq = lambda r: pl.BlockSpec((None, r, width), lambda b, s, pt: (b, 0, 0))
    const = lambda a: pl.BlockSpec(a.shape, lambda b, s, pt: (0, 0))
    grid_spec = pltpu.PrefetchScalarGridSpec(
        num_scalar_prefetch=1,
        grid=(nseq, n_pages // npg),
        in_specs=[per_seq(rows), per_seq(8), per_seq(8)]
        + [page_spec(p) for p in range(npg)] + [page_spec(p) for p in range(npg)]
        + [const(slope_rows), const(t_rows), const(lam_p), const(subln)],
        out_specs=per_seq(8),
        scratch_shapes=[
            pltpu.VMEM((rows, 1), F32),
            pltpu.VMEM((rows, 1), F32),
            pltpu.VMEM((rows, width), F32),
        ],
    )
    return pl.pallas_call(
        functools.partial(_diff_sample_kernel, past=past, lam_init=lam_init),
        out_shape=jax.ShapeDtypeStruct((nseq, 8, width), F32),
        grid_spec=grid_spec,
        compiler_params=_params("parallel", "arbitrary"),
        name="diff_sample",
    )(page_table, qrows, knew, vnew, *([cache_k] * npg), *([cache_v] * npg),
      slope_rows, t_rows, lam_p, subln)


def _to_token_major(x):
    s, t, n = x.shape
    return jnp.transpose(x, (1, 0, 2)).reshape(t * s, n)


def _to_seq_major(x, seqs):
    n = x.shape[-1]
    return jnp.transpose(x.reshape(-1, seqs, n), (1, 0, 2))


def _pad_tokens(x, seqs):
    xs = _to_seq_major(x, seqs)
    return jnp.pad(xs, ((0, 0), (0, 8 - xs.shape[1]), (0, 0)))


def kernel(x_prompt, x_sample, cache_win_k, cache_win_v, cache_diff_k, cache_diff_v, page_table,
           c_prompt, c_sample, ada_w, ada_b, norm_g, w_qkv_a, w_o_a, sinks_a,
           w_qkv_b, w_o_b, lambda_b, subln_b, w_gu, w_down):
    batch, seq, d = x_prompt.shape
    nseq, ntok, _ = x_sample.shape
    depth = ada_w.shape[0]
    assert d == D_MODEL and ntok == 4 and depth == 2
    assert cache_diff_k.shape[0] == 1 and cache_win_k.shape[0] == 1
    n_pool = cache_diff_k.shape[1]
    prompt_tiles = seq // ROW_TILE
    sample_tiles = ntok

    mods = _adaln(jnp.concatenate([c_prompt, c_sample], axis=0), ada_w, ada_b)
    xp = x_prompt.reshape(batch * seq, d)
    xs = _to_token_major(x_sample)

    a_nk = A_KV_HEADS * HEAD_DIM
    b_nk = B_KV_HEADS * B_DK
    outs = {}

    layer = 0
    mod_p = mods[layer, :batch].reshape(batch, 1, N_MOD * d)
    mod_s = mods[layer, batch:].reshape(1, nseq, N_MOD * d)
    g = norm_g[layer]
    wqkv = w_qkv_a[0].astype(BF16)
    wo = w_o_a[0].astype(BF16)
    wgu = w_gu[layer].astype(BF16)
    wd = w_down[layer].astype(BF16)

    q, k, v, _, _ = _qkv(xp, mod_p, g, wqkv, D_MODEL, a_nk, ROW_TILE, prompt_tiles)
    o = _win_prompt(q, k, v, sinks_a[0], batch, seq)
    xp = _out_ffn(o, xp, mod_p, g, wo, wgu, wd, ROW_TILE, prompt_tiles)
    k4 = k.reshape(batch, seq, A_KV_HEADS, HEAD_DIM)
    v4 = v.reshape(batch, seq, A_KV_HEADS, HEAD_DIM)
    outs["wkp"] = k4[:, -WINDOW:][None]
    outs["wvp"] = v4[:, -WINDOW:][None]

    q, k, v, _, _ = _qkv(xs, mod_s, g, wqkv, D_MODEL, a_nk, SAMPLE_TILE, sample_tiles)
    q5 = q.reshape(ntok, nseq, A_KV_HEADS, A_GROUP, HEAD_DIM)
    q5 = jnp.transpose(q5, (1, 2, 3, 0, 4))
    eye = jnp.eye(A_KV_HEADS, dtype=BF16)
    qrows = (q5[:, :, :, :, None, :] * eye[None, :, None, None, :, None]).reshape(nseq, A_HEADS * 4, a_nk)
    kbuf = cache_win_k[0].reshape(nseq, WINDOW, a_nk)
    vbuf = cache_win_v[0].reshape(nseq, WINDOW, a_nk)
    knew = _pad_tokens(k, nseq)
    vnew = _pad_tokens(v, nseq)
    a_slopes = jnp.asarray([_a_slope(h) for h in range(A_HEADS)], F32)
    sink_rows = jnp.repeat(sinks_a[0].astype(F32), 4).reshape(A_HEADS * 4, 1)
    slope_rows = jnp.repeat(a_slopes, 4).reshape(A_HEADS * 4, 1)
    r16 = _win_sample(qrows, kbuf, knew, vbuf, vnew, sink_rows, slope_rows)
    o = r16.reshape(nseq, A_GROUP, ntok, A_KV_HEADS, HEAD_DIM)
    o = jnp.transpose(o, (2, 0, 3, 1, 4)).reshape(ntok * nseq, D_MODEL).astype(BF16)
    xs = _out_ffn(o, xs, mod_s, g, wo, wgu, wd, SAMPLE_TILE, sample_tiles)
    outs["wks"] = jnp.concatenate([kbuf[:, ntok:], knew[:, :ntok]], axis=1).reshape(
        1, nseq, WINDOW, A_KV_HEADS, HEAD_DIM)
    outs["wvs"] = jnp.concatenate([vbuf[:, ntok:], vnew[:, :ntok]], axis=1).reshape(
        1, nseq, WINDOW, A_KV_HEADS, HEAD_DIM)

    layer = 1
    lam_init = 0.8 - 0.6 * math.exp(-0.3 * layer)
    mod_p = mods[layer, :batch].reshape(batch, 1, N_MOD * d)
    mod_s = mods[layer, batch:].reshape(1, nseq, N_MOD * d)
    g = norm_g[layer]
    wqkv = w_qkv_b[0].astype(BF16)
    wo = w_o_b[0].astype(BF16)
    wgu = w_gu[layer].astype(BF16)
    wd = w_down[layer].astype(BF16)
    lam_p = lambda_b[0].astype(F32)
    subln = subln_b[0].astype(F32).reshape(1, B_DK)
    b_slopes = jnp.asarray([_b_slope(h) for h in range(B_HEADS)], F32)

    q, k, v, kb, vb = _qkv(xp, mod_p, g, wqkv, D_MODEL, b_nk, ROW_TILE, prompt_tiles)
    o = _diff_prompt(q, kb, vb, b_slopes, lam_p, subln, batch, seq, lam_init)
    xp = _out_ffn(o, xp, mod_p, g, wo, wgu, wd, ROW_TILE, prompt_tiles)
    outs["dkp"] = k.reshape(1, batch, seq, B_KV_HEADS, B_DK)
    outs["dvp"] = v.reshape(1, batch, seq, B_KV_HEADS, B_DK)

    q, k, v, _, _ = _qkv(xs, mod_s, g, wqkv, D_MODEL, b_nk, SAMPLE_TILE, sample_tiles)
    q6 = q.reshape(ntok, nseq, B_KV_HEADS, B_GROUP, 2, HEAD_DIM)
    q6 = jnp.transpose(q6, (1, 4, 2, 3, 0, 5))
    eye_j = jnp.eye(B_KV_HEADS, dtype=BF16)
    eye_m = jnp.eye(2, dtype=BF16)
    qrows = (q6[:, :, :, :, :, None, None, :]
             * eye_j[None, None, :, None, None, :, None, None]
             * eye_m[None, :, None, None, None, None, :, None]).reshape(nseq, 2 * B_HEADS * 4, b_nk)
    knew = _pad_tokens(k, nseq)
    vnew = _pad_tokens(v, nseq)
    slope_rows = jnp.tile(jnp.repeat(b_slopes, 4), 2).reshape(2 * B_HEADS * 4, 1)
    t_rows = jnp.tile(jnp.arange(4, dtype=F32), 2 * B_HEADS).reshape(2 * B_HEADS * 4, 1)
    ck = cache_diff_k.reshape(n_pool, PAGE_SIZE, b_nk)
    cv = cache_diff_v.reshape(n_pool, PAGE_SIZE, b_nk)
    r8 = _diff_sample(page_table, qrows, knew, vnew, ck, cv, slope_rows, t_rows, lam_p, subln, lam_init)
    o = r8.reshape(nseq, B_GROUP, ntok, B_KV_HEADS, B_DK)
    o = jnp.transpose(o, (2, 0, 3, 1, 4)).reshape(ntok * nseq, D_MODEL).astype(BF16)
    xs = _out_ffn(o, xs, mod_s, g, wo, wgu, wd, SAMPLE_TILE, sample_tiles)
    outs["dks"] = knew[:, :ntok].reshape(1, nseq, ntok, B_KV_HEADS, B_DK)
    outs["dvs"] = vnew[:, :ntok].reshape(1, nseq, ntok, B_KV_HEADS, B_DK)

    y_prompt = xp.reshape(batch, seq, d)
    y_sample = _to_seq_major(xs, nseq)
    return (y_prompt, y_sample, outs["wkp"], outs["wvp"], outs["wks"], outs["wvs"],
            outs["dkp"], outs["dvp"], outs["dks"], outs["dvs"])
```

```python
import functools
import math

import jax
import jax.numpy as jnp
from jax import lax
from jax.experimental import pallas as pl
from jax.experimental.pallas import tpu as pltpu

F32 = jnp.float32
BF16 = jnp.bfloat16

D_MODEL = 1024
HEAD_DIM = 64
WINDOW = 128
A_HEADS = 16
A_KV_HEADS = 4
A_GROUP = 4
B_HEADS = 8
B_KV_HEADS = 4
B_GROUP = 2
B_DK = 128
D_FF = 2816
PAGE_SIZE = 128
RMS_EPS = 1e-6
N_MOD = 6
N_TOK = 4

LANES = 128
MXU_WIDTH = 256
VMEM_LIMIT = 56 * 1024 * 1024
NEG = -1e30
BIG = 1e30

ROW_TILE = 512
SAMPLE_TILE = 128
FF_CHUNK = MXU_WIDTH
ATT_T = 256
PAGES_PER_STEP = 8
PAGE_ROWS = PAGE_SIZE * B_KV_HEADS
NEW_ROWS = N_TOK * B_KV_HEADS
WIN_SAMPLE_SEQS = 8


def _dot(a, b):
    return jnp.dot(a, b, preferred_element_type=F32)


def _dot_nt(a, b):
    return lax.dot_general(a, b, (((1,), (1,)), ((), ())), preferred_element_type=F32)


def _rms(x, g):
    ms = jnp.mean(x * x, axis=-1, keepdims=True)
    return x * lax.rsqrt(ms + RMS_EPS) * g


def _silu(x):
    return x / (1.0 + jnp.exp(-x))


def _params(*sem):
    return pltpu.CompilerParams(dimension_semantics=sem, vmem_limit_bytes=VMEM_LIMIT)


def _adaln_kernel(c_ref, w_ref, b_ref, o_ref):
    s = _silu(c_ref[...]).astype(BF16)
    o_ref[0] = _dot(s, w_ref[0].astype(BF16)) + b_ref[0]


def _adaln(c_all, ada_w, ada_b):
    depth, d, n = ada_w.shape
    rows = c_all.shape[0]
    tn = D_MODEL
    return pl.pallas_call(
        _adaln_kernel,
        out_shape=jax.ShapeDtypeStruct((depth, rows, n), F32),
        grid=(depth, n // tn),
        in_specs=[
            pl.BlockSpec((rows, d), lambda l, t: (0, 0)),
            pl.BlockSpec((1, d, tn), lambda l, t: (l, 0, t)),
            pl.BlockSpec((1, 1, tn), lambda l, t: (l, 0, t)),
        ],
        out_specs=pl.BlockSpec((1, rows, tn), lambda l, t: (l, 0, t)),
        compiler_params=_params("parallel", "parallel"),
        name="adaln",
    )(c_all, ada_w, ada_b.reshape(depth, 1, n))


def _prenorm(x_ref, mod_ref, g_ref):
    shift = mod_ref[0, :, 0:D_MODEL]
    scale = mod_ref[0, :, D_MODEL:2 * D_MODEL]
    return (_rms(x_ref[...], g_ref[0:1, :]) * (1.0 + scale) + shift).astype(BF16)


def _qkv_kernel(x_ref, mod_ref, g_ref, w_ref, q_ref, k_ref, v_ref, *, nq, nk):
    qkv = _dot(_prenorm(x_ref, mod_ref, g_ref), w_ref[...])
    q_ref[...] = (qkv[:, :nq] * HEAD_DIM ** -0.5).astype(BF16)
    k_ref[...] = qkv[:, nq:nq + nk]
    v_ref[...] = qkv[:, nq + nk:]


def _qkv(x, mod, g, w, nq, nk, tm, tiles_per_group):
    rows = x.shape[0]
    r = mod.shape[1]
    row_spec = lambda n: pl.BlockSpec((tm, n), lambda i: (i, 0))
    return pl.pallas_call(
        functools.partial(_qkv_kernel, nq=nq, nk=nk),
        out_shape=(
            jax.ShapeDtypeStruct((rows, nq), BF16),
            jax.ShapeDtypeStruct((rows, nk), F32),
            jax.ShapeDtypeStruct((rows, nk), F32),
        ),
        grid=(rows // tm,),
        in_specs=[
            row_spec(D_MODEL),
            pl.BlockSpec((1, r, N_MOD * D_MODEL), lambda i: (i // tiles_per_group, 0, 0)),
            pl.BlockSpec(g.shape, lambda i: (0, 0)),
            pl.BlockSpec(w.shape, lambda i: (0, 0)),
        ],
        out_specs=(row_spec(nq), row_spec(nk), row_spec(nk)),
        compiler_params=_params("parallel"),
        name="prenorm_qkv",
    )(x, mod, g, w)


def _qkv_t_kernel(x_ref, mod_ref, g_ref, wkv_ref, wqt_ref, wvt_ref,
                  k_ref, v_ref, kb_ref, qt_ref, vt_ref, *, nk):
    h = _prenorm(x_ref, mod_ref, g_ref)
    kv = _dot(h, wkv_ref[...])
    k = kv[:, :nk]
    k_ref[...] = k
    v_ref[...] = kv[:, nk:]
    kb_ref[...] = k.astype(BF16)
    qt = (_dot_nt(wqt_ref[...], h) * HEAD_DIM ** -0.5).astype(BF16)
    vt = _dot_nt(wvt_ref[...], h).astype(BF16)
    for c in range(qt_ref.shape[0]):
        qt_ref[c] = qt[:, c * ATT_T:(c + 1) * ATT_T]
        vt_ref[c] = vt[:, c * ATT_T:(c + 1) * ATT_T]


def _qkv_t(x, mod, g, wkv, wqt, wvt, nq, nk, tm, tiles_per_group):
    rows = x.shape[0]
    r = mod.shape[1]
    per = tm // ATT_T
    row_spec = lambda n: pl.BlockSpec((tm, n), lambda i: (i, 0))
    const = lambda a: pl.BlockSpec(a.shape, lambda i: (0, 0))
    return pl.pallas_call(
        functools.partial(_qkv_t_kernel, nk=nk),
        out_shape=(
            jax.ShapeDtypeStruct((rows, nk), F32),
            jax.ShapeDtypeStruct((rows, nk), F32),
            jax.ShapeDtypeStruct((rows, nk), BF16),
            jax.ShapeDtypeStruct((rows // ATT_T, nq, ATT_T), BF16),
            jax.ShapeDtypeStruct((rows // ATT_T, nk, ATT_T), BF16),
        ),
        grid=(rows // tm,),
        in_specs=[
            row_spec(D_MODEL),
            pl.BlockSpec((1, r, N_MOD * D_MODEL), lambda i: (i // tiles_per_group, 0, 0)),
            const(g), const(wkv), const(wqt), const(wvt),
        ],
        out_specs=(row_spec(nk), row_spec(nk), row_spec(nk),
                   pl.BlockSpec((per, nq, ATT_T), lambda i: (i, 0, 0)),
                   pl.BlockSpec((per, nk, ATT_T), lambda i: (i, 0, 0))),
        compiler_params=_params("parallel"),
        name="prenorm_qkv_t",
    )(x, mod, g, wkv, wqt, wvt)


def _out_ffn_kernel(o_ref, x_ref, mod_ref, g_ref, wo_ref, wgu_ref, wd_ref, y_ref):
    gate1 = mod_ref[0, :, 2 * D_MODEL:3 * D_MODEL]
    shift2 = mod_ref[0, :, 3 * D_MODEL:4 * D_MODEL]
    scale2 = mod_ref[0, :, 4 * D_MODEL:5 * D_MODEL]
    gate2 = mod_ref[0, :, 5 * D_MODEL:6 * D_MODEL]
    y = _dot(o_ref[...], wo_ref[...])
    x1 = x_ref[...] + gate1 * _rms(y, g_ref[1:2, :])
    h = (_rms(x1, g_ref[2:3, :]) * (1.0 + scale2) + shift2).astype(BF16)
    acc = jnp.zeros(x1.shape, F32)
    for c in range(D_FF // FF_CHUNK):
        lo = c * FF_CHUNK
        gch = _dot(h, wgu_ref[:, lo:lo + FF_CHUNK])
        uch = _dot(h, wgu_ref[:, D_FF + lo:D_FF + lo + FF_CHUNK])
        a = (_silu(gch) * uch).astype(BF16)
        acc = acc + _dot(a, wd_ref[lo:lo + FF_CHUNK, :])
    y_ref[...] = x1 + gate2 * _rms(acc, g_ref[3:4, :])


def _out_ffn(o, x, mod, g, wo, wgu, wd, tm, tiles_per_group):
    rows = x.shape[0]
    r = mod.shape[1]
    row_spec = pl.BlockSpec((tm, D_MODEL), lambda i: (i, 0))
    const = lambda a: pl.BlockSpec(a.shape, lambda i: (0, 0), pipeline_mode=pl.Buffered(1))
    return pl.pallas_call(
        _out_ffn_kernel,
        out_shape=jax.ShapeDtypeStruct((rows, D_MODEL), F32),
        grid=(rows // tm,),
        in_specs=[
            row_spec,
            row_spec,
            pl.BlockSpec((1, r, N_MOD * D_MODEL), lambda i: (i // tiles_per_group, 0, 0)),
            const(g), const(wo), const(wgu), const(wd),
        ],
        out_specs=row_spec,
        compiler_params=_params("parallel"),
        name="out_ffn",
    )(o, x, mod, g, wo, wgu, wd)


def _a_slope(h):
    return 2.0 ** (-8.0 * (h + 1) / A_HEADS)


def _b_slope(h):
    return 2.0 ** (-8.0 * (h + 1) / B_HEADS)


def _half_lane_pair(x, kv_head, lo):
    rolled = pltpu.roll(x, HEAD_DIM, 1)
    first, second = (x, rolled) if kv_head % 2 == 0 else (rolled, x)
    return (jnp.where(lo, first, 0.0).astype(BF16), jnp.where(lo, 0.0, second).astype(BF16))


def _win_prompt_kernel(sink_ref, q_ref, kc_ref, kp_ref, vc_ref, vp_ref, o_ref):
    i = pl.program_id(1)
    kk = jnp.concatenate([kp_ref[...], kc_ref[...]], axis=0)
    vv = jnp.concatenate([vp_ref[...], vc_ref[...]], axis=0)
    row = lax.broadcasted_iota(jnp.int32, (WINDOW, 2 * WINDOW), 0)
    col = lax.broadcasted_iota(jnp.int32, (WINDOW, 2 * WINDOW), 1)
    rel = row - col + WINDOW
    valid = (rel >= 0) & (rel < WINDOW) & ((col >= WINDOW) | (i > 0))
    relm = jnp.where(valid, rel.astype(F32), BIG)
    lo = lax.broadcasted_iota(jnp.int32, (2 * WINDOW, LANES), 1) < HEAD_DIM
    for j in range(A_KV_HEADS):
        sl = slice((j // 2) * LANES, (j // 2 + 1) * LANES)
        k_pair = _half_lane_pair(kk[:, sl], j, lo)
        v_pair = _half_lane_pair(vv[:, sl], j, lo)
        for p in range(A_GROUP // 2):
            qsl = slice((2 * j + p) * LANES, (2 * j + p + 1) * LANES)
            qs = q_ref[:, qsl]
            out = None
            for half in range(2):
                h = A_GROUP * j + 2 * p + half
                s = _dot_nt(qs, k_pair[half]) - _a_slope(h) * relm
                sink = sink_ref[h]
                m = jnp.maximum(jnp.max(s, axis=-1, keepdims=True), sink)
                pe = jnp.exp(s - m)
                l = jnp.sum(pe, axis=-1, keepdims=True) + jnp.exp(sink - m)
                o = _dot(pe.astype(BF16), v_pair[half]) / l
                out = o if out is None else out + o
            o_ref[:, qsl] = out.astype(BF16)


def _win_prompt(q, k, v, sinks, batch, seq):
    nb = seq // WINDOW
    nk = A_KV_HEADS * HEAD_DIM
    cur = lambda b, i: (b * nb + i, 0)
    prev = lambda b, i: (b * nb + jnp.maximum(i - 1, 0), 0)
    smem = pl.BlockSpec(memory_space=pltpu.SMEM)
    return pl.pallas_call(
        _win_prompt_kernel,
        out_shape=jax.ShapeDtypeStruct(q.shape, BF16),
        grid=(batch, nb),
        in_specs=[
            smem,
            pl.BlockSpec((WINDOW, D_MODEL), cur),
            pl.BlockSpec((WINDOW, nk), cur),
            pl.BlockSpec((WINDOW, nk), prev),
            pl.BlockSpec((WINDOW, nk), cur),
            pl.BlockSpec((WINDOW, nk), prev),
        ],
        out_specs=pl.BlockSpec((WINDOW, D_MODEL), cur),
        compiler_params=_params("parallel", "parallel"),
        name="win_prompt",
    )(sinks, q, k, k, v, v)


def _win_sample_kernel(q_ref, kbuf_ref, knew_ref, vbuf_ref, vnew_ref, sink_ref, slope_ref, o_ref):
    rows = A_HEADS * N_TOK
    keys = 2 * WINDOW
    r = lax.broadcasted_iota(jnp.int32, (rows, keys), 0)
    pos = lax.broadcasted_iota(jnp.int32, (rows, keys), 1)
    rel = (r % N_TOK) + WINDOW - pos
    valid = (rel >= 0) & (rel < WINDOW)
    bias = jnp.where(valid, -slope_ref[...] * rel.astype(F32), NEG)
    nk = A_KV_HEADS * HEAD_DIM
    own = (lax.broadcasted_iota(jnp.int32, (rows, nk), 1) // HEAD_DIM
           == lax.broadcasted_iota(jnp.int32, (rows, nk), 0) // (A_GROUP * N_TOK))
    sink = sink_ref[...]
    pad = jnp.zeros((keys - WINDOW - 8, nk), F32)
    for b in range(WIN_SAMPLE_SEQS):
        kall = jnp.concatenate([kbuf_ref[b], knew_ref[b], pad], axis=0).astype(BF16)
        vall = jnp.concatenate([vbuf_ref[b], vnew_ref[b], pad], axis=0).astype(BF16)
        s = _dot_nt(q_ref[b], kall) + bias
        m = jnp.maximum(jnp.max(s, axis=-1, keepdims=True), sink)
        pe = jnp.exp(s - m)
        l = jnp.sum(pe, axis=-1, keepdims=True) + jnp.exp(sink - m)
        o = jnp.where(own, _dot(pe.astype(BF16), vall) / l, 0.0)
        o_ref[b] = o[0:16] + o[16:32] + o[32:48] + o[48:64]


def _win_sample(qrows, kbuf, knew, vbuf, vnew, sink_rows, slope_rows):
    nseq = qrows.shape[0]
    sb = WIN_SAMPLE_SEQS
    nk = A_KV_HEADS * HEAD_DIM
    rows = A_HEADS * N_TOK
    blk = lambda r, c: pl.BlockSpec((sb, r, c), lambda i: (i, 0, 0))
    col = pl.BlockSpec((rows, 1), lambda i: (0, 0))
    return pl.pallas_call(
        _win_sample_kernel,
        out_shape=jax.ShapeDtypeStruct((nseq, A_GROUP * N_TOK, nk), F32),
        grid=(nseq // sb,),
        in_specs=[blk(rows, nk), blk(WINDOW, nk), blk(8, nk), blk(WINDOW, nk), blk(8, nk), col, col],
        out_specs=blk(A_GROUP * N_TOK, nk),
        compiler_params=_params("parallel"),
        name="win_sample",
    )(qrows, kbuf, knew, vbuf, vnew, sink_rows, slope_rows)


def _lambda(lam_ref, lam_init):
    lp = lam_ref[...]
    e1 = jnp.exp(jnp.sum(lp[0:1] * lp[1:2], axis=-1, keepdims=True))
    e2 = jnp.exp(jnp.sum(lp[2:3] * lp[3:4], axis=-1, keepdims=True))
    return e1 - e2 + lam_init


def _diff_prompt_kernel(slope_ref, qt_ref, k_ref, vt_ref, lam_ref, sg_ref, o_ref,
                        qs_ref, bias_ref, m_ref, l_ref, acc_ref, *, lam_init):
    t = ATT_T
    nblk = 2 * B_GROUP
    w = nblk * t
    j = pl.program_id(1)
    i = pl.program_id(2)
    first_map = lax.broadcasted_iota(jnp.int32, (B_DK, t), 0) < HEAD_DIM
    key = lax.broadcasted_iota(jnp.int32, (t, t), 0)
    qry = lax.broadcasted_iota(jnp.int32, (t, t), 1)
    dist = (key - qry).astype(F32)
    zero = jnp.zeros((B_DK, t), BF16)
    for g in range(B_GROUP):
        qg = qt_ref[g * B_DK:(g + 1) * B_DK, :]
        qs_ref[:, g * t:(g + 1) * t] = jnp.where(first_map, qg, zero)
        qs_ref[:, (B_GROUP + g) * t:(B_GROUP + g + 1) * t] = jnp.where(first_map, zero, qg)
        bg = slope_ref[B_GROUP * j + g] * dist
        bias_ref[:, g * t:(g + 1) * t] = bg
        bias_ref[:, (B_GROUP + g) * t:(B_GROUP + g + 1) * t] = bg
    m_ref[...] = jnp.full(m_ref.shape, NEG, F32)
    l_ref[...] = jnp.zeros(l_ref.shape, F32)
    acc_ref[...] = jnp.zeros(acc_ref.shape, F32)
    colg = (lax.broadcasted_iota(jnp.int32, (1, w), 1) // t) % B_GROUP
    srow = jnp.where(colg == 0, slope_ref[B_GROUP * j], slope_ref[B_GROUP * j + 1])

    def chunk(c, masked):
        start = pl.multiple_of(c * t, t)
        kc = k_ref[pl.ds(start, t), :]
        vc = vt_ref[c]
        s = _dot(kc, qs_ref[...]) + bias_ref[...]
        if masked:
            kk = lax.broadcasted_iota(jnp.int32, (t, w), 0)
            qq = lax.broadcasted_iota(jnp.int32, (t, w), 1) % t
            s = jnp.where(kk <= qq, s, NEG)
        off = srow * ((c - i) * t).astype(F32)
        m_old = m_ref[...]
        m_new = jnp.maximum(m_old, jnp.max(s, axis=0, keepdims=True) + off)
        pe = jnp.exp(s - (m_new - off))
        alpha = jnp.exp(m_old - m_new)
        l_ref[...] = alpha * l_ref[...] + jnp.sum(pe, axis=0, keepdims=True)
        acc_ref[...] = alpha * acc_ref[...] + _dot(vc, pe.astype(BF16))
        m_ref[...] = m_new

    def body(c, carry):
        chunk(c, False)
        return carry

    lax.fori_loop(0, i, body, 0)
    chunk(i, True)

    lam = _lambda(lam_ref, lam_init)
    a = acc_ref[...] / l_ref[...]
    for g in range(B_GROUP):
        o = a[:, g * t:(g + 1) * t] - lam * a[:, (B_GROUP + g) * t:(B_GROUP + g + 1) * t]
        ms = jnp.mean(o * o, axis=0, keepdims=True)
        o = o * lax.rsqrt(ms + RMS_EPS) * sg_ref[...] * (1.0 - lam_init)
        o_ref[:, g * B_DK:(g + 1) * B_DK] = o.T.astype(BF16)


def _diff_prompt(qt, kb, vt, slopes, lam_p, subln_col, batch, seq, lam_init):
    t = ATT_T
    nq = seq // t
    nblk = 2 * B_GROUP
    width = B_GROUP * B_DK
    smem = pl.BlockSpec(memory_space=pltpu.SMEM)
    const = lambda a: pl.BlockSpec(a.shape, lambda b, j, i: (0, 0))
    o_spec = pl.BlockSpec((t, width), lambda b, j, i: (b * nq + i, j))
    return pl.pallas_call(
        functools.partial(_diff_prompt_kernel, lam_init=lam_init),
        out_shape=jax.ShapeDtypeStruct((batch * seq, B_HEADS * B_DK), BF16),
        grid=(batch, B_KV_HEADS, nq),
        in_specs=[
            smem,
            pl.BlockSpec((None, width, t), lambda b, j, i: (b * nq + i, j, 0)),
            pl.BlockSpec((seq, B_DK), lambda b, j, i: (b, j)),
            pl.BlockSpec((nq, B_DK, t), lambda b, j, i: (b, j, 0)),
            const(lam_p), const(subln_col),
        ],
        out_specs=o_spec,
        scratch_shapes=[
            pltpu.VMEM((B_DK, nblk * t), BF16),
            pltpu.VMEM((t, nblk * t), F32),
            pltpu.VMEM((1, nblk * t), F32),
            pltpu.VMEM((1, nblk * t), F32),
            pltpu.VMEM((B_DK, nblk * t), F32),
        ],
        compiler_params=_params("parallel", "parallel", "parallel"),
        name="diff_prompt",
    )(slopes, qt, kb, vt, lam_p, subln_col)


def _diff_sample_kernel(pt_ref, q_ref, kn_ref, vn_ref, *rest, past, lam_init):
    del pt_ref
    npg = PAGES_PER_STEP
    k_refs = rest[:npg]
    v_refs = rest[npg:2 * npg]
    bias_ref, nbias_ref, slope_ref, t_ref, lam_ref, sg_ref, o_ref, m_ref, l_ref, acc_ref = rest[2 * npg:]
    step = pl.program_id(1)
    rows = 2 * B_HEADS * N_TOK

    @pl.when(step == 0)
    def _():
        m_ref[...] = jnp.full(m_ref.shape, NEG, F32)
        l_ref[...] = jnp.zeros(l_ref.shape, F32)
        acc_ref[...] = jnp.zeros(acc_ref.shape, F32)

    q = q_ref[...]

    def update(ks, vs, bias, off):
        width = ks[0].shape[0]
        parts = [_dot_nt(q, kp.astype(BF16)) for kp in ks]
        s = (parts[0] if len(parts) == 1 else jnp.concatenate(parts, axis=1)) + bias
        m_old = m_ref[...]
        m_new = jnp.maximum(m_old, jnp.max(s, axis=-1, keepdims=True) + off)
        pe = jnp.exp(s - (m_new - off))
        alpha = jnp.exp(m_old - m_new)
        l_ref[...] = alpha * l_ref[...] + jnp.sum(pe, axis=-1, keepdims=True)
        pe = pe.astype(BF16)
        pv = None
        for n, vp in enumerate(vs):
            term = _dot(pe[:, n * width:(n + 1) * width], vp.astype(BF16))
            pv = term if pv is None else pv + term
        acc_ref[...] = alpha * acc_ref[...] + pv
        m_ref[...] = m_new

    base = (step * (npg * PAGE_SIZE)).astype(F32)
    off = slope_ref[...] * (base - float(past) - t_ref[...])
    update([r[...] for r in k_refs], [r[...] for r in v_refs], bias_ref[...], off)

    @pl.when(step == pl.num_programs(1) - 1)
    def _():
        pad = jnp.zeros((LANES - NEW_ROWS, B_DK), F32)
        knew = jnp.concatenate([kn_ref[...], pad], axis=0)
        vnew = jnp.concatenate([vn_ref[...], pad], axis=0)
        update([knew], [vnew], nbias_ref[...], 0.0)

        lam = _lambda(lam_ref, lam_init)
        a = acc_ref[...] / l_ref[...]
        half = rows // 2
        dm = a[0:half] - lam * a[half:rows]
        o_ref[...] = _rms(dm, sg_ref[...]) * (1.0 - lam_init)


def _diff_sample(page_table, qrows, knew, vnew, cache_k, cache_v, bias, nbias, slope_rows, t_rows,
                 lam_p, subln, lam_init):
    nseq, n_pages = page_table.shape
    npg = PAGES_PER_STEP
    rows = 2 * B_HEADS * N_TOK
    past = n_pages * PAGE_SIZE

    def page_spec(p):
        return pl.BlockSpec((None, PAGE_ROWS, B_DK), lambda b, s, pt: (pt[b, s * npg + p], 0, 0))

    per_seq = lambda r: pl.BlockSpec((None, r, B_DK), lambda b, s, pt: (b, 0, 0))
    const = lambda a: pl.BlockSpec(a.shape, lambda b, s, pt: (0, 0))
    grid_spec = pltpu.PrefetchScalarGridSpec(
        num_scalar_prefetch=1,
        grid=(nseq, n_pages // npg),
        in_specs=[per_seq(rows), per_seq(NEW_ROWS), per_seq(NEW_ROWS)]
        + [page_spec(p) for p in range(npg)] + [page_spec(p) for p in range(npg)]
        + [const(bias), const(nbias), const(slope_rows), const(t_rows), const(lam_p), const(subln)],
        out_specs=per_seq(rows // 2),
        scratch_shapes=[
            pltpu.VMEM((rows, 1), F32),
            pltpu.VMEM((rows, 1), F32),
            pltpu.VMEM((rows, B_DK), F32),
        ],
    )
    return pl.pallas_call(
        functools.partial(_diff_sample_kernel, past=past, lam_init=lam_init),
        out_shape=jax.ShapeDtypeStruct((nseq, rows // 2, B_DK), F32),
        grid_spec=grid_spec,
        compiler_params=_params("parallel", "arbitrary"),
        name="diff_sample",
    )(page_table, qrows, knew, vnew, *([cache_k] * npg), *([cache_v] * npg),
      bias, nbias, slope_rows, t_rows, lam_p, subln)


def _diff_sample_bias(slopes):
    rows = 2 * B_HEADS * N_TOK
    r = jnp.arange(rows)
    row_j = (r % (B_HEADS * N_TOK)) // (B_GROUP * N_TOK)
    row_t = r % N_TOK
    row_slope = slopes[(r % (B_HEADS * N_TOK)) // N_TOK]
    col = jnp.arange(PAGES_PER_STEP * PAGE_ROWS)
    same = row_j[:, None] == (col % B_KV_HEADS)[None, :]
    bias = jnp.where(same, row_slope[:, None] * (col // B_KV_HEADS).astype(F32)[None, :], NEG)
    ncol = jnp.arange(LANES)
    npos = ncol // B_KV_HEADS
    ok = ((row_j[:, None] == (ncol % B_KV_HEADS)[None, :]) & (npos[None, :] <= row_t[:, None])
          & (ncol < NEW_ROWS)[None, :])
    nbias = jnp.where(ok, row_slope[:, None] * (npos[None, :] - row_t[:, None]).astype(F32), NEG)
    return (bias.astype(F32), nbias.astype(F32), row_slope.reshape(rows, 1).astype(F32),
            row_t.reshape(rows, 1).astype(F32))


def _to_token_major(x):
    s, t, n = x.shape
    return jnp.transpose(x, (1, 0, 2)).reshape(t * s, n)


def _to_seq_major(x, seqs):
    n = x.shape[-1]
    return jnp.transpose(x.reshape(-1, seqs, n), (1, 0, 2))


def _pad_tokens(x, seqs):
    xs = _to_seq_major(x, seqs)
    return jnp.pad(xs, ((0, 0), (0, 8 - xs.shape[1]), (0, 0)))


def kernel(x_prompt, x_sample, cache_win_k, cache_win_v, cache_diff_k, cache_diff_v, page_table,
           c_prompt, c_sample, ada_w, ada_b, norm_g, w_qkv_a, w_o_a, sinks_a,
           w_qkv_b, w_o_b, lambda_b, subln_b, w_gu, w_down):
    batch, seq, d = x_prompt.shape
    nseq, ntok, _ = x_sample.shape
    depth = ada_w.shape[0]
    assert d == D_MODEL and ntok == N_TOK and depth == 2
    assert cache_diff_k.shape[0] == 1 and cache_win_k.shape[0] == 1
    n_pool = cache_diff_k.shape[1]
    prompt_tiles = seq // ROW_TILE
    sample_tiles = ntok

    mods = _adaln(jnp.concatenate([c_prompt, c_sample], axis=0), ada_w, ada_b)
    xp = x_prompt.reshape(batch * seq, d)
    xs = _to_token_major(x_sample)

    a_nk = A_KV_HEADS * HEAD_DIM
    b_nk = B_KV_HEADS * B_DK
    outs = {}

    layer = 0
    mod_p = mods[layer, :batch].reshape(batch, 1, N_MOD * d)
    mod_s = mods[layer, batch:].reshape(1, nseq, N_MOD * d)
    g = norm_g[layer]
    wqkv = w_qkv_a[0].astype(BF16)
    wo = w_o_a[0].astype(BF16)
    wgu = w_gu[layer].astype(BF16)
    wd = w_down[layer].astype(BF16)

    q, k, v = _qkv(xp, mod_p, g, wqkv, D_MODEL, a_nk, ROW_TILE, prompt_tiles)
    o = _win_prompt(q, k, v, sinks_a[0], batch, seq)
    xp = _out_ffn(o, xp, mod_p, g, wo, wgu, wd, ROW_TILE, prompt_tiles)
    k4 = k.reshape(batch, seq, A_KV_HEADS, HEAD_DIM)
    v4 = v.reshape(batch, seq, A_KV_HEADS, HEAD_DIM)
    outs["wkp"] = k4[:, -WINDOW:][None]
    outs["wvp"] = v4[:, -WINDOW:][None]

    q, k, v = _qkv(xs, mod_s, g, wqkv, D_MODEL, a_nk, SAMPLE_TILE, sample_tiles)
    q5 = q.reshape(ntok, nseq, A_KV_HEADS, A_GROUP, HEAD_DIM)
    q5 = jnp.transpose(q5, (1, 2, 3, 0, 4))
    eye = jnp.eye(A_KV_HEADS, dtype=BF16)
    qrows = (q5[:, :, :, :, None, :] * eye[None, :, None, None, :, None]).reshape(nseq, A_HEADS * ntok, a_nk)
    kbuf = cache_win_k[0].reshape(nseq, WINDOW, a_nk)
    vbuf = cache_win_v[0].reshape(nseq, WINDOW, a_nk)
    knew = _pad_tokens(k, nseq)
    vnew = _pad_tokens(v, nseq)
    a_slopes = jnp.asarray([_a_slope(h) for h in range(A_HEADS)], F32)
    sink_rows = jnp.repeat(sinks_a[0].astype(F32), ntok).reshape(A_HEADS * ntok, 1)
    slope_rows = jnp.repeat(a_slopes, ntok).reshape(A_HEADS * ntok, 1)
    r16 = _win_sample(qrows, kbuf, knew, vbuf, vnew, sink_rows, slope_rows)
    o = r16.reshape(nseq, A_GROUP, ntok, A_KV_HEADS, HEAD_DIM)
    o = jnp.transpose(o, (2, 0, 3, 1, 4)).reshape(ntok * nseq, D_MODEL).astype(BF16)
    xs = _out_ffn(o, xs, mod_s, g, wo, wgu, wd, SAMPLE_TILE, sample_tiles)
    outs["wks"] = jnp.concatenate([kbuf[:, ntok:], knew[:, :ntok]], axis=1).reshape(
        1, nseq, WINDOW, A_KV_HEADS, HEAD_DIM)
    outs["wvs"] = jnp.concatenate([vbuf[:, ntok:], vnew[:, :ntok]], axis=1).reshape(
        1, nseq, WINDOW, A_KV_HEADS, HEAD_DIM)

    layer = 1
    lam_init = 0.8 - 0.6 * math.exp(-0.3 * layer)
    mod_p = mods[layer, :batch].reshape(batch, 1, N_MOD * d)
    mod_s = mods[layer, batch:].reshape(1, nseq, N_MOD * d)
    g = norm_g[layer]
    wqkv = w_qkv_b[0].astype(BF16)
    wo = w_o_b[0].astype(BF16)
    wgu = w_gu[layer].astype(BF16)
    wd = w_down[layer].astype(BF16)
    lam_p = lambda_b[0].astype(F32)
    subln = subln_b[0].astype(F32)
    b_slopes = jnp.asarray([_b_slope(h) for h in range(B_HEADS)], F32)

    wkv = wqkv[:, D_MODEL:]
    wqt = wqkv[:, :D_MODEL].T
    wvt = wqkv[:, D_MODEL + b_nk:].T
    k, v, kb, qt, vt = _qkv_t(xp, mod_p, g, wkv, wqt, wvt, D_MODEL, b_nk, ROW_TILE, prompt_tiles)
    o = _diff_prompt(qt, kb, vt, b_slopes, lam_p, subln.reshape(B_DK, 1), batch, seq, lam_init)
    xp = _out_ffn(o, xp, mod_p, g, wo, wgu, wd, ROW_TILE, prompt_tiles)
    outs["dkp"] = k.reshape(1, batch, seq, B_KV_HEADS, B_DK)
    outs["dvp"] = v.reshape(1, batch, seq, B_KV_HEADS, B_DK)

    q, k, v = _qkv(xs, mod_s, g, wqkv, D_MODEL, b_nk, SAMPLE_TILE, sample_tiles)
    q6 = q.reshape(ntok, nseq, B_KV_HEADS, B_GROUP, 2, HEAD_DIM)
    q6 = jnp.transpose(q6, (1, 4, 2, 3, 0, 5))
    eye_m = jnp.eye(2, dtype=BF16)
    qrows = (q6[:, :, :, :, :, None, :] * eye_m[None, :, None, None, None, :, None]).reshape(
        nseq, 2 * B_HEADS * ntok, B_DK)
    ks = _to_seq_major(k, nseq)
    vs = _to_seq_major(v, nseq)
    bias, nbias, slope_rows, t_rows = _diff_sample_bias(b_slopes)
    ck = cache_diff_k.reshape(n_pool, PAGE_ROWS, B_DK)
    cv = cache_diff_v.reshape(n_pool, PAGE_ROWS, B_DK)
    r32 = _diff_sample(page_table, qrows, ks.reshape(nseq, NEW_ROWS, B_DK), vs.reshape(nseq, NEW_ROWS, B_DK),
                       ck, cv, bias, nbias, slope_rows, t_rows, lam_p, subln.reshape(1, B_DK), lam_init)
    o = r32.reshape(nseq, B_KV_HEADS, B_GROUP, ntok, B_DK)
    o = jnp.transpose(o, (3, 0, 1, 2, 4)).reshape(ntok * nseq, D_MODEL).astype(BF16)
    xs = _out_ffn(o, xs, mod_s, g, wo, wgu, wd, SAMPLE_TILE, sample_tiles)
    outs["dks"] = ks.reshape(1, nseq, ntok, B_KV_HEADS, B_DK)
    outs["dvs"] = vs.reshape(1, nseq, ntok, B_KV_HEADS, B_DK)

    y_prompt = xp.reshape(batch, seq, d)
    y_sample = _to_seq_major(xs, nseq)
    return (y_prompt, y_sample, outs["wkp"], outs["wvp"], outs["wks"], outs["wvs"],
            outs["dkp"], outs["dvp"], outs["dks"], outs["dvs"])
```

```python
import functools
import math

import jax
import jax.numpy as jnp
from jax import lax
from jax.experimental import pallas as pl
from jax.experimental.pallas import tpu as pltpu

F32 = jnp.float32
BF16 = jnp.bfloat16

D_MODEL = 1024
HEAD_DIM = 64
WINDOW = 128
A_HEADS = 16
A_KV_HEADS = 4
A_GROUP = 4
B_HEADS = 8
B_KV_HEADS = 4
B_GROUP = 2
B_DK = 128
D_FF = 2816
PAGE_SIZE = 128
RMS_EPS = 1e-6
N_MOD = 6
N_TOK = 4

LANES = 128
MXU_WIDTH = 256
VMEM_LIMIT = 56 * 1024 * 1024
NEG = -1e30
BIG = 1e30

ROW_TILE = 512
SAMPLE_TILE = 128
FF_CHUNK = MXU_WIDTH
ATT_T = 256
PAGES_PER_STEP = 16
RING_SLOTS = 3
PAGE_ROWS = PAGE_SIZE * B_KV_HEADS
NEW_ROWS = N_TOK * B_KV_HEADS
WIN_SAMPLE_SEQS = 8


def _dot(a, b):
    return jnp.dot(a, b, preferred_element_type=F32)


def _dot_nt(a, b):
    return lax.dot_general(a, b, (((1,), (1,)), ((), ())), preferred_element_type=F32)


def _rms(x, g):
    ms = jnp.mean(x * x, axis=-1, keepdims=True)
    return x * lax.rsqrt(ms + RMS_EPS) * g


def _silu(x):
    return x / (1.0 + jnp.exp(-x))


def _params(*sem):
    return pltpu.CompilerParams(dimension_semantics=sem, vmem_limit_bytes=VMEM_LIMIT)


def _adaln_kernel(c_ref, w_ref, b_ref, o_ref):
    s = _silu(c_ref[...]).astype(BF16)
    o_ref[0] = _dot(s, w_ref[0].astype(BF16)) + b_ref[0]


def _adaln(c_all, ada_w, ada_b):
    depth, d, n = ada_w.shape
    rows = c_all.shape[0]
    tn = D_MODEL
    return pl.pallas_call(
        _adaln_kernel,
        out_shape=jax.ShapeDtypeStruct((depth, rows, n), F32),
        grid=(depth, n // tn),
        in_specs=[
            pl.BlockSpec((rows, d), lambda l, t: (0, 0)),
            pl.BlockSpec((1, d, tn), lambda l, t: (l, 0, t)),
            pl.BlockSpec((1, 1, tn), lambda l, t: (l, 0, t)),
        ],
        out_specs=pl.BlockSpec((1, rows, tn), lambda l, t: (l, 0, t)),
        compiler_params=_params("parallel", "parallel"),
        name="adaln",
    )(c_all, ada_w, ada_b.reshape(depth, 1, n))


def _prenorm(x_ref, mod_ref, g_ref):
    shift = mod_ref[0, :, 0:D_MODEL]
    scale = mod_ref[0, :, D_MODEL:2 * D_MODEL]
    return (_rms(x_ref[...], g_ref[0:1, :]) * (1.0 + scale) + shift).astype(BF16)


def _qkv_kernel(x_ref, mod_ref, g_ref, w_ref, q_ref, k_ref, v_ref, *, nq, nk):
    qkv = _dot(_prenorm(x_ref, mod_ref, g_ref), w_ref[...])
    q_ref[...] = (qkv[:, :nq] * HEAD_DIM ** -0.5).astype(BF16)
    k_ref[...] = qkv[:, nq:nq + nk]
    v_ref[...] = qkv[:, nq + nk:]


def _qkv(x, mod, g, w, nq, nk, tm, tiles_per_group):
    rows = x.shape[0]
    r = mod.shape[1]
    row_spec = lambda n: pl.BlockSpec((tm, n), lambda i: (i, 0))
    return pl.pallas_call(
        functools.partial(_qkv_kernel, nq=nq, nk=nk),
        out_shape=(
            jax.ShapeDtypeStruct((rows, nq), BF16),
            jax.ShapeDtypeStruct((rows, nk), F32),
            jax.ShapeDtypeStruct((rows, nk), F32),
        ),
        grid=(rows // tm,),
        in_specs=[
            row_spec(D_MODEL),
            pl.BlockSpec((1, r, N_MOD * D_MODEL), lambda i: (i // tiles_per_group, 0, 0)),
            pl.BlockSpec(g.shape, lambda i: (0, 0)),
            pl.BlockSpec(w.shape, lambda i: (0, 0)),
        ],
        out_specs=(row_spec(nq), row_spec(nk), row_spec(nk)),
        compiler_params=_params("parallel"),
        name="prenorm_qkv",
    )(x, mod, g, w)


def _qkv_t_kernel(x_ref, mod_ref, g_ref, wkv_ref, wqt_ref, wvt_ref,
                  k_ref, v_ref, kb_ref, qt_ref, vt_ref, *, nk):
    h = _prenorm(x_ref, mod_ref, g_ref)
    kv = _dot(h, wkv_ref[...])
    k = kv[:, :nk]
    k_ref[...] = k
    v_ref[...] = kv[:, nk:]
    kb_ref[...] = k.astype(BF16)
    qt = (_dot_nt(wqt_ref[...], h) * HEAD_DIM ** -0.5).astype(BF16)
    vt = _dot_nt(wvt_ref[...], h).astype(BF16)
    for c in range(qt_ref.shape[0]):
        qt_ref[c] = qt[:, c * ATT_T:(c + 1) * ATT_T]
        vt_ref[c] = vt[:, c * ATT_T:(c + 1) * ATT_T]


def _qkv_t(x, mod, g, wkv, wqt, wvt, nq, nk, tm, tiles_per_group):
    rows = x.shape[0]
    r = mod.shape[1]
    per = tm // ATT_T
    row_spec = lambda n: pl.BlockSpec((tm, n), lambda i: (i, 0))
    const = lambda a: pl.BlockSpec(a.shape, lambda i: (0, 0))
    return pl.pallas_call(
        functools.partial(_qkv_t_kernel, nk=nk),
        out_shape=(
            jax.ShapeDtypeStruct((rows, nk), F32),
            jax.ShapeDtypeStruct((rows, nk), F32),
            jax.ShapeDtypeStruct((rows, nk), BF16),
            jax.ShapeDtypeStruct((rows // ATT_T, nq, ATT_T), BF16),
            jax.ShapeDtypeStruct((rows // ATT_T, nk, ATT_T), BF16),
        ),
        grid=(rows // tm,),
        in_specs=[
            row_spec(D_MODEL),
            pl.BlockSpec((1, r, N_MOD * D_MODEL), lambda i: (i // tiles_per_group, 0, 0)),
            const(g), const(wkv), const(wqt), const(wvt),
        ],
        out_specs=(row_spec(nk), row_spec(nk), row_spec(nk),
                   pl.BlockSpec((per, nq, ATT_T), lambda i: (i, 0, 0)),
                   pl.BlockSpec((per, nk, ATT_T), lambda i: (i, 0, 0))),
        compiler_params=_params("parallel"),
        name="prenorm_qkv_t",
    )(x, mod, g, wkv, wqt, wvt)


def _out_ffn_kernel(o_ref, x_ref, mod_ref, g_ref, wo_ref, wgu_ref, wd_ref, y_ref):
    gate1 = mod_ref[0, :, 2 * D_MODEL:3 * D_MODEL]
    shift2 = mod_ref[0, :, 3 * D_MODEL:4 * D_MODEL]
    scale2 = mod_ref[0, :, 4 * D_MODEL:5 * D_MODEL]
    gate2 = mod_ref[0, :, 5 * D_MODEL:6 * D_MODEL]
    y = _dot(o_ref[...], wo_ref[...])
    x1 = x_ref[...] + gate1 * _rms(y, g_ref[1:2, :])
    h = (_rms(x1, g_ref[2:3, :]) * (1.0 + scale2) + shift2).astype(BF16)
    acc = jnp.zeros(x1.shape, F32)
    for c in range(D_FF // FF_CHUNK):
        lo = c * FF_CHUNK
        gch = _dot(h, wgu_ref[:, lo:lo + FF_CHUNK])
        uch = _dot(h, wgu_ref[:, D_FF + lo:D_FF + lo + FF_CHUNK])
        a = (_silu(gch) * uch).astype(BF16)
        acc = acc + _dot(a, wd_ref[lo:lo + FF_CHUNK, :])
    y_ref[...] = x1 + gate2 * _rms(acc, g_ref[3:4, :])


def _out_ffn(o, x, mod, g, wo, wgu, wd, tm, tiles_per_group):
    rows = x.shape[0]
    r = mod.shape[1]
    row_spec = pl.BlockSpec((tm, D_MODEL), lambda i: (i, 0))
    const = lambda a: pl.BlockSpec(a.shape, lambda i: (0, 0), pipeline_mode=pl.Buffered(1))
    return pl.pallas_call(
        _out_ffn_kernel,
        out_shape=jax.ShapeDtypeStruct((rows, D_MODEL), F32),
        grid=(rows // tm,),
        in_specs=[
            row_spec,
            row_spec,
            pl.BlockSpec((1, r, N_MOD * D_MODEL), lambda i: (i // tiles_per_group, 0, 0)),
            const(g), const(wo), const(wgu), const(wd),
        ],
        out_specs=row_spec,
        compiler_params=_params("parallel"),
        name="out_ffn",
    )(o, x, mod, g, wo, wgu, wd)


def _a_slope(h):
    return 2.0 ** (-8.0 * (h + 1) / A_HEADS)


def _b_slope(h):
    return 2.0 ** (-8.0 * (h + 1) / B_HEADS)


def _half_lane_pair(x, kv_head, lo):
    rolled = pltpu.roll(x, HEAD_DIM, 1)
    first, second = (x, rolled) if kv_head % 2 == 0 else (rolled, x)
    return (jnp.where(lo, first, 0.0).astype(BF16), jnp.where(lo, 0.0, second).astype(BF16))


def _win_prompt_kernel(sink_ref, q_ref, kc_ref, kp_ref, vc_ref, vp_ref, o_ref):
    i = pl.program_id(1)
    kk = jnp.concatenate([kp_ref[...], kc_ref[...]], axis=0)
    vv = jnp.concatenate([vp_ref[...], vc_ref[...]], axis=0)
    row = lax.broadcasted_iota(jnp.int32, (WINDOW, 2 * WINDOW), 0)
    col = lax.broadcasted_iota(jnp.int32, (WINDOW, 2 * WINDOW), 1)
    rel = row - col + WINDOW
    valid = (rel >= 0) & (rel < WINDOW) & ((col >= WINDOW) | (i > 0))
    relm = jnp.where(valid, rel.astype(F32), BIG)
    lo = lax.broadcasted_iota(jnp.int32, (2 * WINDOW, LANES), 1) < HEAD_DIM
    for j in range(A_KV_HEADS):
        sl = slice((j // 2) * LANES, (j // 2 + 1) * LANES)
        k_pair = _half_lane_pair(kk[:, sl], j, lo)
        v_pair = _half_lane_pair(vv[:, sl], j, lo)
        for p in range(A_GROUP // 2):
            qsl = slice((2 * j + p) * LANES, (2 * j + p + 1) * LANES)
            qs = q_ref[:, qsl]
            out = None
            for half in range(2):
                h = A_GROUP * j + 2 * p + half
                s = _dot_nt(qs, k_pair[half]) - _a_slope(h) * relm
                sink = sink_ref[h]
                m = jnp.maximum(jnp.max(s, axis=-1, keepdims=True), sink)
                pe = jnp.exp(s - m)
                l = jnp.sum(pe, axis=-1, keepdims=True) + jnp.exp(sink - m)
                o = _dot(pe.astype(BF16), v_pair[half]) / l
                out = o if out is None else out + o
            o_ref[:, qsl] = out.astype(BF16)


def _win_prompt(q, k, v, sinks, batch, seq):
    nb = seq // WINDOW
    nk = A_KV_HEADS * HEAD_DIM
    cur = lambda b, i: (b * nb + i, 0)
    prev = lambda b, i: (b * nb + jnp.maximum(i - 1, 0), 0)
    smem = pl.BlockSpec(memory_space=pltpu.SMEM)
    return pl.pallas_call(
        _win_prompt_kernel,
        out_shape=jax.ShapeDtypeStruct(q.shape, BF16),
        grid=(batch, nb),
        in_specs=[
            smem,
            pl.BlockSpec((WINDOW, D_MODEL), cur),
            pl.BlockSpec((WINDOW, nk), cur),
            pl.BlockSpec((WINDOW, nk), prev),
            pl.BlockSpec((WINDOW, nk), cur),
            pl.BlockSpec((WINDOW, nk), prev),
        ],
        out_specs=pl.BlockSpec((WINDOW, D_MODEL), cur),
        compiler_params=_params("parallel", "parallel"),
        name="win_prompt",
    )(sinks, q, k, k, v, v)


def _win_sample_kernel(q_ref, kbuf_ref, knew_ref, vbuf_ref, vnew_ref, sink_ref, slope_ref, o_ref):
    rows = A_HEADS * N_TOK
    keys = 2 * WINDOW
    r = lax.broadcasted_iota(jnp.int32, (rows, keys), 0)
    pos = lax.broadcasted_iota(jnp.int32, (rows, keys), 1)
    rel = (r % N_TOK) + WINDOW - pos
    valid = (rel >= 0) & (rel < WINDOW)
    bias = jnp.where(valid, -slope_ref[...] * rel.astype(F32), NEG)
    nk = A_KV_HEADS * HEAD_DIM
    own = (lax.broadcasted_iota(jnp.int32, (rows, nk), 1) // HEAD_DIM
           == lax.broadcasted_iota(jnp.int32, (rows, nk), 0) // (A_GROUP * N_TOK))
    sink = sink_ref[...]
    pad = jnp.zeros((keys - WINDOW - 8, nk), F32)
    for b in range(WIN_SAMPLE_SEQS):
        kall = jnp.concatenate([kbuf_ref[b], knew_ref[b], pad], axis=0).astype(BF16)
        vall = jnp.concatenate([vbuf_ref[b], vnew_ref[b], pad], axis=0).astype(BF16)
        s = _dot_nt(q_ref[b], kall) + bias
        m = jnp.maximum(jnp.max(s, axis=-1, keepdims=True), sink)
        pe = jnp.exp(s - m)
        l = jnp.sum(pe, axis=-1, keepdims=True) + jnp.exp(sink - m)
        o = jnp.where(own, _dot(pe.astype(BF16), vall) / l, 0.0)
        o_ref[b] = o[0:16] + o[16:32] + o[32:48] + o[48:64]


def _win_sample(qrows, kbuf, knew, vbuf, vnew, sink_rows, slope_rows):
    nseq = qrows.shape[0]
    sb = WIN_SAMPLE_SEQS
    nk = A_KV_HEADS * HEAD_DIM
    rows = A_HEADS * N_TOK
    blk = lambda r, c: pl.BlockSpec((sb, r, c), lambda i: (i, 0, 0))
    col = pl.BlockSpec((rows, 1), lambda i: (0, 0))
    return pl.pallas_call(
        _win_sample_kernel,
        out_shape=jax.ShapeDtypeStruct((nseq, A_GROUP * N_TOK, nk), F32),
        grid=(nseq // sb,),
        in_specs=[blk(rows, nk), blk(WINDOW, nk), blk(8, nk), blk(WINDOW, nk), blk(8, nk), col, col],
        out_specs=blk(A_GROUP * N_TOK, nk),
        compiler_params=_params("parallel"),
        name="win_sample",
    )(qrows, kbuf, knew, vbuf, vnew, sink_rows, slope_rows)


def _lambda(lam_ref, lam_init):
    lp = lam_ref[...]
    e1 = jnp.exp(jnp.sum(lp[0:1] * lp[1:2], axis=-1, keepdims=True))
    e2 = jnp.exp(jnp.sum(lp[2:3] * lp[3:4], axis=-1, keepdims=True))
    return e1 - e2 + lam_init


def _diff_prompt_kernel(slope_ref, qt_ref, k_ref, vt_ref, lam_ref, sg_ref, o_ref,
                        qs_ref, bias_ref, dbias_ref, m_ref, l_ref, acc_ref, *, lam_init):
    t = ATT_T
    nblk = 2 * B_GROUP
    w = nblk * t
    j = pl.program_id(1)
    i = pl.program_id(2)
    first_map = lax.broadcasted_iota(jnp.int32, (B_DK, t), 0) < HEAD_DIM
    zero = jnp.zeros((B_DK, t), BF16)
    for g in range(B_GROUP):
        qg = qt_ref[g * B_DK:(g + 1) * B_DK, :]
        qs_ref[:, g * t:(g + 1) * t] = jnp.where(first_map, qg, zero)
        qs_ref[:, (B_GROUP + g) * t:(B_GROUP + g + 1) * t] = jnp.where(first_map, zero, qg)

    @pl.when(i == 0)
    def _():
        key = lax.broadcasted_iota(jnp.int32, (t, t), 0)
        qry = lax.broadcasted_iota(jnp.int32, (t, t), 1)
        dist = (key - qry).astype(F32)
        for g in range(B_GROUP):
            bg = slope_ref[B_GROUP * j + g] * dist
            dg = jnp.where(key <= qry, bg, NEG)
            for blk in (g, B_GROUP + g):
                bias_ref[:, blk * t:(blk + 1) * t] = bg
                dbias_ref[:, blk * t:(blk + 1) * t] = dg

    m_ref[...] = jnp.full(m_ref.shape, NEG, F32)
    l_ref[...] = jnp.zeros(l_ref.shape, F32)
    acc_ref[...] = jnp.zeros(acc_ref.shape, F32)
    colg = (lax.broadcasted_iota(jnp.int32, (1, w), 1) // t) % B_GROUP
    srow = jnp.where(colg == 0, slope_ref[B_GROUP * j], slope_ref[B_GROUP * j + 1])

    def chunk(c, bias):
        start = pl.multiple_of(c * t, t)
        kc = k_ref[pl.ds(start, t), :]
        vc = vt_ref[c]
        s = _dot(kc, qs_ref[...]) + bias[...]
        off = srow * ((c - i) * t).astype(F32)
        m_old = m_ref[...]
        m_new = jnp.maximum(m_old, jnp.max(s, axis=0, keepdims=True) + off)
        pe = jnp.exp(s - (m_new - off))
        alpha = jnp.exp(m_old - m_new)
        l_ref[...] = alpha * l_ref[...] + jnp.sum(pe, axis=0, keepdims=True)
        acc_ref[...] = alpha * acc_ref[...] + _dot(vc, pe.astype(BF16))
        m_ref[...] = m_new

    def body(c, carry):
        chunk(c, bias_ref)
        return carry

    lax.fori_loop(0, i, body, 0)
    chunk(i, dbias_ref)

    lam = _lambda(lam_ref, lam_init)
    a = acc_ref[...] / l_ref[...]
    for g in range(B_GROUP):
        o = a[:, g * t:(g + 1) * t] - lam * a[:, (B_GROUP + g) * t:(B_GROUP + g + 1) * t]
        ms = jnp.mean(o * o, axis=0, keepdims=True)
        o = o * lax.rsqrt(ms + RMS_EPS) * sg_ref[...] * (1.0 - lam_init)
        o_ref[:, g * B_DK:(g + 1) * B_DK] = o.T.astype(BF16)


def _diff_prompt(qt, kb, vt, slopes, lam_p, subln_col, batch, seq, lam_init):
    t = ATT_T
    nq = seq // t
    nblk = 2 * B_GROUP
    width = B_GROUP * B_DK
    smem = pl.BlockSpec(memory_space=pltpu.SMEM)
    const = lambda a: pl.BlockSpec(a.shape, lambda b, j, i: (0, 0))
    o_spec = pl.BlockSpec((t, width), lambda b, j, i: (b * nq + i, j))
    return pl.pallas_call(
        functools.partial(_diff_prompt_kernel, lam_init=lam_init),
        out_shape=jax.ShapeDtypeStruct((batch * seq, B_HEADS * B_DK), BF16),
        grid=(batch, B_KV_HEADS, nq),
        in_specs=[
            smem,
            pl.BlockSpec((None, width, t), lambda b, j, i: (b * nq + i, j, 0)),
            pl.BlockSpec((seq, B_DK), lambda b, j, i: (b, j)),
            pl.BlockSpec((nq, B_DK, t), lambda b, j, i: (b, j, 0)),
            const(lam_p), const(subln_col),
        ],
        out_specs=o_spec,
        scratch_shapes=[
            pltpu.VMEM((B_DK, nblk * t), BF16),
            pltpu.VMEM((t, nblk * t), F32),
            pltpu.VMEM((t, nblk * t), F32),
            pltpu.VMEM((1, nblk * t), F32),
            pltpu.VMEM((1, nblk * t), F32),
            pltpu.VMEM((B_DK, nblk * t), F32),
        ],
        compiler_params=_params("parallel", "parallel", "arbitrary"),
        name="diff_prompt",
    )(slopes, qt, kb, vt, lam_p, subln_col)


def _diff_sample_kernel(pt_ref, q_ref, kn_ref, vn_ref, ck_hbm, cv_hbm, bias_ref, nbias_ref, slope_ref, t_ref,
                        lam_ref, sg_ref, o_ref, kring, vring, sem, m_ref, l_ref, acc_ref, *, past, lam_init):
    npg = PAGES_PER_STEP
    step = pl.program_id(1)
    steps_per_seq = pl.num_programs(1)
    n = pl.program_id(0) * steps_per_seq + step
    total = pl.num_programs(0) * steps_per_seq
    rows = 2 * B_HEADS * N_TOK

    def page_copies(stream_step):
        seq = stream_step // steps_per_seq
        first = (stream_step % steps_per_seq) * npg
        slot = stream_step % RING_SLOTS
        out = []
        for p in range(npg):
            page = pt_ref[seq, first + p]
            out.append(pltpu.make_async_copy(ck_hbm.at[page], kring.at[slot, p], sem.at[0, slot]))
            out.append(pltpu.make_async_copy(cv_hbm.at[page], vring.at[slot, p], sem.at[1, slot]))
        return out

    @pl.when(n == 0)
    def _():
        for ahead in range(RING_SLOTS - 1):
            for c in page_copies(ahead):
                c.start()

    @pl.when(n + (RING_SLOTS - 1) < total)
    def _():
        for c in page_copies(n + (RING_SLOTS - 1)):
            c.start()

    @pl.when(step == 0)
    def _():
        m_ref[...] = jnp.full(m_ref.shape, NEG, F32)
        l_ref[...] = jnp.zeros(l_ref.shape, F32)
        acc_ref[...] = jnp.zeros(acc_ref.shape, F32)

    q = q_ref[...]

    def update(ks, vs, bias, off):
        width = ks[0].shape[0]
        parts = [_dot_nt(q, kp.astype(BF16)) for kp in ks]
        s = (parts[0] if len(parts) == 1 else jnp.concatenate(parts, axis=1)) + bias
        m_old = m_ref[...]
        m_new = jnp.maximum(m_old, jnp.max(s, axis=-1, keepdims=True) + off)
        pe = jnp.exp(s - (m_new - off))
        alpha = jnp.exp(m_old - m_new)
        l_ref[...] = alpha * l_ref[...] + jnp.sum(pe, axis=-1, keepdims=True)
        pe = pe.astype(BF16)
        pv = None
        for idx, vp in enumerate(vs):
            term = _dot(pe[:, idx * width:(idx + 1) * width], vp.astype(BF16))
            pv = term if pv is None else pv + term
        acc_ref[...] = alpha * acc_ref[...] + pv
        m_ref[...] = m_new

    for c in page_copies(n):
        c.wait()
    slot = n % RING_SLOTS
    base = (step * (npg * PAGE_SIZE)).astype(F32)
    off = slope_ref[...] * (base - float(past) - t_ref[...])
    update([kring[slot, p] for p in range(npg)], [vring[slot, p] for p in range(npg)], bias_ref[...], off)

    @pl.when(step == steps_per_seq - 1)
    def _():
        pad = jnp.zeros((LANES - NEW_ROWS, B_DK), F32)
        knew = jnp.concatenate([kn_ref[...], pad], axis=0)
        vnew = jnp.concatenate([vn_ref[...], pad], axis=0)
        update([knew], [vnew], nbias_ref[...], 0.0)

        lam = _lambda(lam_ref, lam_init)
        a = acc_ref[...] / l_ref[...]
        half = rows // 2
        dm = a[0:half] - lam * a[half:rows]
        o_ref[...] = _rms(dm, sg_ref[...]) * (1.0 - lam_init)


def _diff_sample(page_table, qrows, knew, vnew, cache_k, cache_v, bias, nbias, slope_rows, t_rows,
                 lam_p, subln, lam_init):
    nseq, n_pages = page_table.shape
    npg = PAGES_PER_STEP
    rows = 2 * B_HEADS * N_TOK
    past = n_pages * PAGE_SIZE
    assert n_pages % npg == 0 and nseq * (n_pages // npg) >= RING_SLOTS - 1

    per_seq = lambda r: pl.BlockSpec((None, r, B_DK), lambda b, s, pt: (b, 0, 0))
    const = lambda a: pl.BlockSpec(a.shape, lambda b, s, pt: (0, 0))
    hbm = pl.BlockSpec(memory_space=pl.ANY)
    grid_spec = pltpu.PrefetchScalarGridSpec(
        num_scalar_prefetch=1,
        grid=(nseq, n_pages // npg),
        in_specs=[per_seq(rows), per_seq(NEW_ROWS), per_seq(NEW_ROWS), hbm, hbm,
                  const(bias), const(nbias), const(slope_rows), const(t_rows), const(lam_p), const(subln)],
        out_specs=per_seq(rows // 2),
        scratch_shapes=[
            pltpu.VMEM((RING_SLOTS, npg, PAGE_ROWS, B_DK), F32),
            pltpu.VMEM((RING_SLOTS, npg, PAGE_ROWS, B_DK), F32),
            pltpu.SemaphoreType.DMA((2, RING_SLOTS)),
            pltpu.VMEM((rows, 1), F32),
            pltpu.VMEM((rows, 1), F32),
            pltpu.VMEM((rows, B_DK), F32),
        ],
    )
    return pl.pallas_call(
        functools.partial(_diff_sample_kernel, past=past, lam_init=lam_init),
        out_shape=jax.ShapeDtypeStruct((nseq, rows // 2, B_DK), F32),
        grid_spec=grid_spec,
        compiler_params=_params("arbitrary", "arbitrary"),
        name="diff_sample",
    )(page_table, qrows, knew, vnew, cache_k, cache_v, bias, nbias, slope_rows, t_rows, lam_p, subln)


def _diff_sample_bias(slopes):
    rows = 2 * B_HEADS * N_TOK
    r = jnp.arange(rows)
    row_j = (r % (B_HEADS * N_TOK)) // (B_GROUP * N_TOK)
    row_t = r % N_TOK
    row_slope = slopes[(r % (B_HEADS * N_TOK)) // N_TOK]
    col = jnp.arange(PAGES_PER_STEP * PAGE_ROWS)
    same = row_j[:, None] == (col % B_KV_HEADS)[None, :]
    bias = jnp.where(same, row_slope[:, None] * (col // B_KV_HEADS).astype(F32)[None, :], NEG)
    ncol = jnp.arange(LANES)
    npos = ncol // B_KV_HEADS
    ok = ((row_j[:, None] == (ncol % B_KV_HEADS)[None, :]) & (npos[None, :] <= row_t[:, None])
          & (ncol < NEW_ROWS)[None, :])
    nbias = jnp.where(ok, row_slope[:, None] * (npos[None, :] - row_t[:, None]).astype(F32), NEG)
    return (bias.astype(F32), nbias.astype(F32), row_slope.reshape(rows, 1).astype(F32),
            row_t.reshape(rows, 1).astype(F32))


def _to_token_major(x):
    s, t, n = x.shape
    return jnp.transpose(x, (1, 0, 2)).reshape(t * s, n)


def _to_seq_major(x, seqs):
    n = x.shape[-1]
    return jnp.transpose(x.reshape(-1, seqs, n), (1, 0, 2))


def _pad_tokens(x, seqs):
    xs = _to_seq_major(x, seqs)
    return jnp.pad(xs, ((0, 0), (0, 8 - xs.shape[1]), (0, 0)))


def kernel(x_prompt, x_sample, cache_win_k, cache_win_v, cache_diff_k, cache_diff_v, page_table,
           c_prompt, c_sample, ada_w, ada_b, norm_g, w_qkv_a, w_o_a, sinks_a,
           w_qkv_b, w_o_b, lambda_b, subln_b, w_gu, w_down):
    batch, seq, d = x_prompt.shape
    nseq, ntok, _ = x_sample.shape
    depth = ada_w.shape[0]
    assert d == D_MODEL and ntok == N_TOK and depth == 2
    assert cache_diff_k.shape[0] == 1 and cache_win_k.shape[0] == 1
    n_pool = cache_diff_k.shape[1]
    prompt_tiles = seq // ROW_TILE
    sample_tiles = ntok

    mods = _adaln(jnp.concatenate([c_prompt, c_sample], axis=0), ada_w, ada_b)
    xp = x_prompt.reshape(batch * seq, d)
    xs = _to_token_major(x_sample)

    a_nk = A_KV_HEADS * HEAD_DIM
    b_nk = B_KV_HEADS * B_DK
    outs = {}

    layer = 0
    mod_p = mods[layer, :batch].reshape(batch, 1, N_MOD * d)
    mod_s = mods[layer, batch:].reshape(1, nseq, N_MOD * d)
    g = norm_g[layer]
    wqkv = w_qkv_a[0].astype(BF16)
    wo = w_o_a[0].astype(BF16)
    wgu = w_gu[layer].astype(BF16)
    wd = w_down[layer].astype(BF16)

    q, k, v = _qkv(xp, mod_p, g, wqkv, D_MODEL, a_nk, ROW_TILE, prompt_tiles)
    o = _win_prompt(q, k, v, sinks_a[0], batch, seq)
    xp = _out_ffn(o, xp, mod_p, g, wo, wgu, wd, ROW_TILE, prompt_tiles)
    k4 = k.reshape(batch, seq, A_KV_HEADS, HEAD_DIM)
    v4 = v.reshape(batch, seq, A_KV_HEADS, HEAD_DIM)
    outs["wkp"] = k4[:, -WINDOW:][None]
    outs["wvp"] = v4[:, -WINDOW:][None]

    q, k, v = _qkv(xs, mod_s, g, wqkv, D_MODEL, a_nk, SAMPLE_TILE, sample_tiles)
    q5 = q.reshape(ntok, nseq, A_KV_HEADS, A_GROUP, HEAD_DIM)
    q5 = jnp.transpose(q5, (1, 2, 3, 0, 4))
    eye = jnp.eye(A_KV_HEADS, dtype=BF16)
    qrows = (q5[:, :, :, :, None, :] * eye[None, :, None, None, :, None]).reshape(nseq, A_HEADS * ntok, a_nk)
    kbuf = cache_win_k[0].reshape(nseq, WINDOW, a_nk)
    vbuf = cache_win_v[0].reshape(nseq, WINDOW, a_nk)
    knew = _pad_tokens(k, nseq)
    vnew = _pad_tokens(v, nseq)
    a_slopes = jnp.asarray([_a_slope(h) for h in range(A_HEADS)], F32)
    sink_rows = jnp.repeat(sinks_a[0].astype(F32), ntok).reshape(A_HEADS * ntok, 1)
    slope_rows = jnp.repeat(a_slopes, ntok).reshape(A_HEADS * ntok, 1)
    r16 = _win_sample(qrows, kbuf, knew, vbuf, vnew, sink_rows, slope_rows)
    o = r16.reshape(nseq, A_GROUP, ntok, A_KV_HEADS, HEAD_DIM)
    o = jnp.transpose(o, (2, 0, 3, 1, 4)).reshape(ntok * nseq, D_MODEL).astype(BF16)
    xs = _out_ffn(o, xs, mod_s, g, wo, wgu, wd, SAMPLE_TILE, sample_tiles)
    outs["wks"] = jnp.concatenate([kbuf[:, ntok:], knew[:, :ntok]], axis=1).reshape(
        1, nseq, WINDOW, A_KV_HEADS, HEAD_DIM)
    outs["wvs"] = jnp.concatenate([vbuf[:, ntok:], vnew[:, :ntok]], axis=1).reshape(
        1, nseq, WINDOW, A_KV_HEADS, HEAD_DIM)

    layer = 1
    lam_init = 0.8 - 0.6 * math.exp(-0.3 * layer)
    mod_p = mods[layer, :batch].reshape(batch, 1, N_MOD * d)
    mod_s = mods[layer, batch:].reshape(1, nseq, N_MOD * d)
    g = norm_g[layer]
    wqkv = w_qkv_b[0].astype(BF16)
    wo = w_o_b[0].astype(BF16)
    wgu = w_gu[layer].astype(BF16)
    wd = w_down[layer].astype(BF16)
    lam_p = lambda_b[0].astype(F32)
    subln = subln_b[0].astype(F32)
    b_slopes = jnp.asarray([_b_slope(h) for h in range(B_HEADS)], F32)

    wkv = wqkv[:, D_MODEL:]
    wqt = wqkv[:, :D_MODEL].T
    wvt = wqkv[:, D_MODEL + b_nk:].T
    k, v, kb, qt, vt = _qkv_t(xp, mod_p, g, wkv, wqt, wvt, D_MODEL, b_nk, ROW_TILE, prompt_tiles)
    o = _diff_prompt(qt, kb, vt, b_slopes, lam_p, subln.reshape(B_DK, 1), batch, seq, lam_init)
    xp = _out_ffn(o, xp, mod_p, g, wo, wgu, wd, ROW_TILE, prompt_tiles)
    outs["dkp"] = k.reshape(1, batch, seq, B_KV_HEADS, B_DK)
    outs["dvp"] = v.reshape(1, batch, seq, B_KV_HEADS, B_DK)

    q, k, v = _qkv(xs, mod_s, g, wqkv, D_MODEL, b_nk, SAMPLE_TILE, sample_tiles)
    q6 = q.reshape(ntok, nseq, B_KV_HEADS, B_GROUP, 2, HEAD_DIM)
    q6 = jnp.transpose(q6, (1, 4, 2, 3, 0, 5))
    eye_m = jnp.eye(2, dtype=BF16)
    qrows = (q6[:, :, :, :, :, None, :] * eye_m[None, :, None, None, None, :, None]).reshape(
        nseq, 2 * B_HEADS * ntok, B_DK)
    ks = _to_seq_major(k, nseq)
    vs = _to_seq_major(v, nseq)
    bias, nbias, slope_rows, t_rows = _diff_sample_bias(b_slopes)
    ck = cache_diff_k.reshape(n_pool, PAGE_ROWS, B_DK)
    cv = cache_diff_v.reshape(n_pool, PAGE_ROWS, B_DK)
    r32 = _diff_sample(page_table, qrows, ks.reshape(nseq, NEW_ROWS, B_DK), vs.reshape(nseq, NEW_ROWS, B_DK),
                       ck, cv, bias, nbias, slope_rows, t_rows, lam_p, subln.reshape(1, B_DK), lam_init)
    o = r32.reshape(nseq, B_KV_HEADS, B_GROUP, ntok, B_DK)
    o = jnp.transpose(o, (3, 0, 1, 2, 4)).reshape(ntok * nseq, D_MODEL).astype(BF16)
    xs = _out_ffn(o, xs, mod_s, g, wo, wgu, wd, SAMPLE_TILE, sample_tiles)
    outs["dks"] = ks.reshape(1, nseq, ntok, B_KV_HEADS, B_DK)
    outs["dvs"] = vs.reshape(1, nseq, ntok, B_KV_HEADS, B_DK)

    y_prompt = xp.reshape(batch, seq, d)
    y_sample = _to_seq_major(xs, nseq)
    return (y_prompt, y_sample, outs["wkp"], outs["wvp"], outs["wks"], outs["wvs"],
            outs["dkp"], outs["dvp"], outs["dks"], outs["dvs"])
```

```python
import functools
import math

import jax
import jax.numpy as jnp
from jax import lax
from jax.experimental import pallas as pl
from jax.experimental.pallas import tpu as pltpu

F32 = jnp.float32
BF16 = jnp.bfloat16

D_MODEL = 1024
HEAD_DIM = 64
WINDOW = 128
A_HEADS = 16
A_KV_HEADS = 4
A_GROUP = 4
B_HEADS = 8
B_KV_HEADS = 4
B_GROUP = 2
B_DK = 128
D_FF = 2816
PAGE_SIZE = 128
RMS_EPS = 1e-6
N_MOD = 6
N_TOK = 4

LANES = 128
MXU_WIDTH = 256
VMEM_LIMIT = 56 * 1024 * 1024
NEG = -1e30
BIG = 1e30

ROW_TILE = 512
SAMPLE_TILE = 128
FF_CHUNK = MXU_WIDTH
ATT_T = 256
PAGES_PER_STEP = 16
RING_SLOTS = 3
PAGE_ROWS = PAGE_SIZE * B_KV_HEADS
NEW_ROWS = N_TOK * B_KV_HEADS
WIN_SAMPLE_SEQS = 8


def _dot(a, b):
    return jnp.dot(a, b, preferred_element_type=F32)


def _dot_nt(a, b):
    return lax.dot_general(a, b, (((1,), (1,)), ((), ())), preferred_element_type=F32)


def _rms(x, g):
    ms = jnp.mean(x * x, axis=-1, keepdims=True)
    return x * lax.rsqrt(ms + RMS_EPS) * g


def _silu(x):
    return x / (1.0 + jnp.exp(-x))


def _params(*sem):
    return pltpu.CompilerParams(dimension_semantics=sem, vmem_limit_bytes=VMEM_LIMIT)


def _adaln_kernel(c_ref, w_ref, b_ref, o_ref):
    s = _silu(c_ref[...]).astype(BF16)
    o_ref[0] = _dot(s, w_ref[0].astype(BF16)) + b_ref[0]


def _adaln(c_all, ada_w, ada_b):
    depth, d, n = ada_w.shape
    rows = c_all.shape[0]
    tn = D_MODEL
    return pl.pallas_call(
        _adaln_kernel,
        out_shape=jax.ShapeDtypeStruct((depth, rows, n), F32),
        grid=(depth, n // tn),
        in_specs=[
            pl.BlockSpec((rows, d), lambda l, t: (0, 0)),
            pl.BlockSpec((1, d, tn), lambda l, t: (l, 0, t)),
            pl.BlockSpec((1, 1, tn), lambda l, t: (l, 0, t)),
        ],
        out_specs=pl.BlockSpec((1, rows, tn), lambda l, t: (l, 0, t)),
        compiler_params=_params("parallel", "parallel"),
        name="adaln",
    )(c_all, ada_w, ada_b.reshape(depth, 1, n))


def _prenorm(x_ref, mod_ref, g_ref):
    shift = mod_ref[0, :, 0:D_MODEL]
    scale = mod_ref[0, :, D_MODEL:2 * D_MODEL]
    return (_rms(x_ref[...], g_ref[0:1, :]) * (1.0 + scale) + shift).astype(BF16)


def _qkv_kernel(x_ref, mod_ref, g_ref, w_ref, q_ref, k_ref, v_ref, *, nq, nk):
    qkv = _dot(_prenorm(x_ref, mod_ref, g_ref), w_ref[...])
    q_ref[...] = (qkv[:, :nq] * HEAD_DIM ** -0.5).astype(BF16)
    k_ref[...] = qkv[:, nq:nq + nk]
    v_ref[...] = qkv[:, nq + nk:]


def _qkv(x, mod, g, w, nq, nk, tm, tiles_per_group):
    rows = x.shape[0]
    r = mod.shape[1]
    row_spec = lambda n: pl.BlockSpec((tm, n), lambda i: (i, 0))
    return pl.pallas_call(
        functools.partial(_qkv_kernel, nq=nq, nk=nk),
        out_shape=(
            jax.ShapeDtypeStruct((rows, nq), BF16),
            jax.ShapeDtypeStruct((rows, nk), F32),
            jax.ShapeDtypeStruct((rows, nk), F32),
        ),
        grid=(rows // tm,),
        in_specs=[
            row_spec(D_MODEL),
            pl.BlockSpec((1, r, N_MOD * D_MODEL), lambda i: (i // tiles_per_group, 0, 0)),
            pl.BlockSpec(g.shape, lambda i: (0, 0)),
            pl.BlockSpec(w.shape, lambda i: (0, 0)),
        ],
        out_specs=(row_spec(nq), row_spec(nk), row_spec(nk)),
        compiler_params=_params("parallel"),
        name="prenorm_qkv",
    )(x, mod, g, w)


def _qkv_t_kernel(x_ref, mod_ref, g_ref, wkv_ref, wqt_ref, wvt_ref,
                  k_ref, v_ref, kb_ref, qt_ref, vt_ref, *, nk):
    h = _prenorm(x_ref, mod_ref, g_ref)
    kv = _dot(h, wkv_ref[...])
    k = kv[:, :nk]
    tm = k.shape[0]
    for jj in range(B_KV_HEADS):
        k_ref[pl.ds(jj, tm, stride=B_KV_HEADS), :] = kv[:, jj * B_DK:(jj + 1) * B_DK]
        v_ref[pl.ds(jj, tm, stride=B_KV_HEADS), :] = kv[:, nk + jj * B_DK:nk + (jj + 1) * B_DK]
    kb_ref[...] = k.astype(BF16)
    qt = (_dot_nt(wqt_ref[...], h) * HEAD_DIM ** -0.5).astype(BF16)
    vt = _dot_nt(wvt_ref[...], h).astype(BF16)
    for c in range(qt_ref.shape[0]):
        qt_ref[c] = qt[:, c * ATT_T:(c + 1) * ATT_T]
        vt_ref[c] = vt[:, c * ATT_T:(c + 1) * ATT_T]


def _qkv_t(x, mod, g, wkv, wqt, wvt, nq, nk, tm, tiles_per_group):
    rows = x.shape[0]
    r = mod.shape[1]
    per = tm // ATT_T
    row_spec = lambda n: pl.BlockSpec((tm, n), lambda i: (i, 0))
    const = lambda a: pl.BlockSpec(a.shape, lambda i: (0, 0))
    return pl.pallas_call(
        functools.partial(_qkv_t_kernel, nk=nk),
        out_shape=(
            jax.ShapeDtypeStruct((rows * B_KV_HEADS, B_DK), F32),
            jax.ShapeDtypeStruct((rows * B_KV_HEADS, B_DK), F32),
            jax.ShapeDtypeStruct((rows, nk), BF16),
            jax.ShapeDtypeStruct((rows // ATT_T, nq, ATT_T), BF16),
            jax.ShapeDtypeStruct((rows // ATT_T, nk, ATT_T), BF16),
        ),
        grid=(rows // tm,),
        in_specs=[
            row_spec(D_MODEL),
            pl.BlockSpec((1, r, N_MOD * D_MODEL), lambda i: (i // tiles_per_group, 0, 0)),
            const(g), const(wkv), const(wqt), const(wvt),
        ],
        out_specs=(pl.BlockSpec((tm * B_KV_HEADS, B_DK), lambda i: (i, 0)),
                   pl.BlockSpec((tm * B_KV_HEADS, B_DK), lambda i: (i, 0)),
                   row_spec(nk),
                   pl.BlockSpec((per, nq, ATT_T), lambda i: (i, 0, 0)),
                   pl.BlockSpec((per, nk, ATT_T), lambda i: (i, 0, 0))),
        compiler_params=_params("parallel"),
        name="prenorm_qkv_t",
    )(x, mod, g, wkv, wqt, wvt)


def _out_ffn_kernel(o_ref, x_ref, mod_ref, g_ref, wo_ref, wgu_ref, wd_ref, y_ref):
    gate1 = mod_ref[0, :, 2 * D_MODEL:3 * D_MODEL]
    shift2 = mod_ref[0, :, 3 * D_MODEL:4 * D_MODEL]
    scale2 = mod_ref[0, :, 4 * D_MODEL:5 * D_MODEL]
    gate2 = mod_ref[0, :, 5 * D_MODEL:6 * D_MODEL]
    y = _dot(o_ref[...], wo_ref[...])
    x1 = x_ref[...] + gate1 * _rms(y, g_ref[1:2, :])
    h = (_rms(x1, g_ref[2:3, :]) * (1.0 + scale2) + shift2).astype(BF16)
    acc = jnp.zeros(x1.shape, F32)
    for c in range(D_FF // FF_CHUNK):
        lo = c * FF_CHUNK
        gch = _dot(h, wgu_ref[:, lo:lo + FF_CHUNK])
        uch = _dot(h, wgu_ref[:, D_FF + lo:D_FF + lo + FF_CHUNK])
        a = (_silu(gch) * uch).astype(BF16)
        acc = acc + _dot(a, wd_ref[lo:lo + FF_CHUNK, :])
    y_ref[...] = x1 + gate2 * _rms(acc, g_ref[3:4, :])


def _out_ffn(o, x, mod, g, wo, wgu, wd, layer, tm, tiles_per_group):
    rows = x.shape[0]
    r = mod.shape[1]
    row_spec = pl.BlockSpec((tm, D_MODEL), lambda i: (i, 0))
    const = lambda a: pl.BlockSpec(a.shape, lambda i: (0, 0), pipeline_mode=pl.Buffered(1))
    of_layer = lambda a: pl.BlockSpec((None,) + a.shape[1:], lambda i: (layer, 0, 0),
                                      pipeline_mode=pl.Buffered(1))
    return pl.pallas_call(
        _out_ffn_kernel,
        out_shape=jax.ShapeDtypeStruct((rows, D_MODEL), F32),
        grid=(rows // tm,),
        in_specs=[
            row_spec,
            row_spec,
            pl.BlockSpec((1, r, N_MOD * D_MODEL), lambda i: (i // tiles_per_group, 0, 0)),
            const(g), const(wo), of_layer(wgu), of_layer(wd),
        ],
        out_specs=row_spec,
        compiler_params=_params("parallel"),
        name="out_ffn",
    )(o, x, mod, g, wo, wgu, wd)


def _a_slope(h):
    return 2.0 ** (-8.0 * (h + 1) / A_HEADS)


def _b_slope(h):
    return 2.0 ** (-8.0 * (h + 1) / B_HEADS)


def _half_lane_pair(x, kv_head, lo):
    rolled = pltpu.roll(x, HEAD_DIM, 1)
    first, second = (x, rolled) if kv_head % 2 == 0 else (rolled, x)
    return (jnp.where(lo, first, 0.0).astype(BF16), jnp.where(lo, 0.0, second).astype(BF16))


def _win_prompt_kernel(sink_ref, q_ref, kc_ref, kp_ref, vc_ref, vp_ref, o_ref):
    i = pl.program_id(1)
    kk = jnp.concatenate([kp_ref[...], kc_ref[...]], axis=0)
    vv = jnp.concatenate([vp_ref[...], vc_ref[...]], axis=0)
    row = lax.broadcasted_iota(jnp.int32, (WINDOW, 2 * WINDOW), 0)
    col = lax.broadcasted_iota(jnp.int32, (WINDOW, 2 * WINDOW), 1)
    rel = row - col + WINDOW
    valid = (rel >= 0) & (rel < WINDOW) & ((col >= WINDOW) | (i > 0))
    relm = jnp.where(valid, rel.astype(F32), BIG)
    lo = lax.broadcasted_iota(jnp.int32, (2 * WINDOW, LANES), 1) < HEAD_DIM
    for j in range(A_KV_HEADS):
        sl = slice((j // 2) * LANES, (j // 2 + 1) * LANES)
        k_pair = _half_lane_pair(kk[:, sl], j, lo)
        v_pair = _half_lane_pair(vv[:, sl], j, lo)
        for p in range(A_GROUP // 2):
            qsl = slice((2 * j + p) * LANES, (2 * j + p + 1) * LANES)
            qs = q_ref[:, qsl]
            out = None
            for half in range(2):
                h = A_GROUP * j + 2 * p + half
                s = _dot_nt(qs, k_pair[half]) - _a_slope(h) * relm
                sink = sink_ref[h]
                m = jnp.maximum(jnp.max(s, axis=-1, keepdims=True), sink)
                pe = jnp.exp(s - m)
                l = jnp.sum(pe, axis=-1, keepdims=True) + jnp.exp(sink - m)
                o = _dot(pe.astype(BF16), v_pair[half]) / l
                out = o if out is None else out + o
            o_ref[:, qsl] = out.astype(BF16)


def _win_prompt(q, k, v, sinks, batch, seq):
    nb = seq // WINDOW
    nk = A_KV_HEADS * HEAD_DIM
    cur = lambda b, i: (b * nb + i, 0)
    prev = lambda b, i: (b * nb + jnp.maximum(i - 1, 0), 0)
    smem = pl.BlockSpec(memory_space=pltpu.SMEM)
    return pl.pallas_call(
        _win_prompt_kernel,
        out_shape=jax.ShapeDtypeStruct(q.shape, BF16),
        grid=(batch, nb),
        in_specs=[
            smem,
            pl.BlockSpec((WINDOW, D_MODEL), cur),
            pl.BlockSpec((WINDOW, nk), cur),
            pl.BlockSpec((WINDOW, nk), prev),
            pl.BlockSpec((WINDOW, nk), cur),
            pl.BlockSpec((WINDOW, nk), prev),
        ],
        out_specs=pl.BlockSpec((WINDOW, D_MODEL), cur),
        compiler_params=_params("parallel", "parallel"),
        name="win_prompt",
    )(sinks, q, k, k, v, v)


def _win_sample_kernel(q_ref, kbuf_ref, knew_ref, vbuf_ref, vnew_ref, sink_ref, slope_ref, o_ref):
    rows = A_HEADS * N_TOK
    keys = 2 * WINDOW
    r = lax.broadcasted_iota(jnp.int32, (rows, keys), 0)
    pos = lax.broadcasted_iota(jnp.int32, (rows, keys), 1)
    rel = (r % N_TOK) + WINDOW - pos
    valid = (rel >= 0) & (rel < WINDOW)
    bias = jnp.where(valid, -slope_ref[...] * rel.astype(F32), NEG)
    nk = A_KV_HEADS * HEAD_DIM
    own = (lax.broadcasted_iota(jnp.int32, (rows, nk), 1) // HEAD_DIM
           == lax.broadcasted_iota(jnp.int32, (rows, nk), 0) // (A_GROUP * N_TOK))
    sink = sink_ref[...]
    pad = jnp.zeros((keys - WINDOW - 8, nk), F32)
    for b in range(WIN_SAMPLE_SEQS):
        kall = jnp.concatenate([kbuf_ref[b], knew_ref[b], pad], axis=0).astype(BF16)
        vall = jnp.concatenate([vbuf_ref[b], vnew_ref[b], pad], axis=0).astype(BF16)
        s = _dot_nt(q_ref[b], kall) + bias
        m = jnp.maximum(jnp.max(s, axis=-1, keepdims=True), sink)
        pe = jnp.exp(s - m)
        l = jnp.sum(pe, axis=-1, keepdims=True) + jnp.exp(sink - m)
        o = jnp.where(own, _dot(pe.astype(BF16), vall) / l, 0.0)
        o_ref[b] = o[0:16] + o[16:32] + o[32:48] + o[48:64]


def _win_sample(qrows, kbuf, knew, vbuf, vnew, sink_rows, slope_rows):
    nseq = qrows.shape[0]
    sb = WIN_SAMPLE_SEQS
    nk = A_KV_HEADS * HEAD_DIM
    rows = A_HEADS * N_TOK
    blk = lambda r, c: pl.BlockSpec((sb, r, c), lambda i: (i, 0, 0))
    col = pl.BlockSpec((rows, 1), lambda i: (0, 0))
    return pl.pallas_call(
        _win_sample_kernel,
        out_shape=jax.ShapeDtypeStruct((nseq, A_GROUP * N_TOK, nk), F32),
        grid=(nseq // sb,),
        in_specs=[blk(rows, nk), blk(WINDOW, nk), blk(8, nk), blk(WINDOW, nk), blk(8, nk), col, col],
        out_specs=blk(A_GROUP * N_TOK, nk),
        compiler_params=_params("parallel"),
        name="win_sample",
    )(qrows, kbuf, knew, vbuf, vnew, sink_rows, slope_rows)


def _lambda(lam_ref, lam_init):
    lp = lam_ref[...]
    e1 = jnp.exp(jnp.sum(lp[0:1] * lp[1:2], axis=-1, keepdims=True))
    e2 = jnp.exp(jnp.sum(lp[2:3] * lp[3:4], axis=-1, keepdims=True))
    return e1 - e2 + lam_init


def _diff_prompt_kernel(slope_ref, qt_ref, k_ref, vt_ref, lam_ref, sg_ref, o_ref,
                        qs_ref, bias_ref, dbias_ref, sa_ref, sb_ref, m_ref, l_ref, acc_ref, *, lam_init):
    t = ATT_T
    nblk = 2 * B_GROUP
    w = nblk * t
    j = pl.program_id(1)
    i = pl.program_id(2)
    first_map = lax.broadcasted_iota(jnp.int32, (B_DK, t), 0) < HEAD_DIM
    zero = jnp.zeros((B_DK, t), BF16)
    for g in range(B_GROUP):
        qg = qt_ref[g * B_DK:(g + 1) * B_DK, :]
        qs_ref[:, g * t:(g + 1) * t] = jnp.where(first_map, qg, zero)
        qs_ref[:, (B_GROUP + g) * t:(B_GROUP + g + 1) * t] = jnp.where(first_map, zero, qg)

    @pl.when(i == 0)
    def _():
        key = lax.broadcasted_iota(jnp.int32, (t, t), 0)
        qry = lax.broadcasted_iota(jnp.int32, (t, t), 1)
        dist = (key - qry).astype(F32)
        for g in range(B_GROUP):
            bg = slope_ref[B_GROUP * j + g] * dist
            dg = jnp.where(key <= qry, bg, NEG)
            for blk in (g, B_GROUP + g):
                bias_ref[:, blk * t:(blk + 1) * t] = bg
                dbias_ref[:, blk * t:(blk + 1) * t] = dg

    m_ref[...] = jnp.full(m_ref.shape, NEG, F32)
    l_ref[...] = jnp.zeros(l_ref.shape, F32)
    acc_ref[...] = jnp.zeros(acc_ref.shape, F32)
    colg = (lax.broadcasted_iota(jnp.int32, (1, w), 1) // t) % B_GROUP
    srow = jnp.where(colg == 0, slope_ref[B_GROUP * j], slope_ref[B_GROUP * j + 1])

    def scores(c):
        start = pl.multiple_of(c * t, t)
        return _dot(k_ref[pl.ds(start, t), :], qs_ref[...])

    def update(s, c):
        off = srow * ((c - i) * t).astype(F32)
        m_old = m_ref[...]
        m_new = jnp.maximum(m_old, jnp.max(s, axis=0, keepdims=True) + off)
        pe = jnp.exp(s - (m_new - off))
        alpha = jnp.exp(m_old - m_new)
        l_ref[...] = alpha * l_ref[...] + jnp.sum(pe, axis=0, keepdims=True)
        acc_ref[...] = alpha * acc_ref[...] + _dot(vt_ref[c], pe.astype(BF16))
        m_ref[...] = m_new

    sa_ref[...] = scores(0)

    def pair(p, carry):
        c0 = 2 * p
        sb_ref[...] = scores(c0 + 1)
        update(sa_ref[...] + bias_ref[...], c0)
        sa_ref[...] = scores(c0 + 2)
        update(sb_ref[...] + bias_ref[...], c0 + 1)
        return carry

    lax.fori_loop(0, i // 2, pair, 0)

    @pl.when(i % 2 == 1)
    def _():
        sb_ref[...] = scores(i)
        update(sa_ref[...] + bias_ref[...], i - 1)
        update(sb_ref[...] + dbias_ref[...], i)

    @pl.when(i % 2 == 0)
    def _():
        update(sa_ref[...] + dbias_ref[...], i)

    lam = _lambda(lam_ref, lam_init)
    a = acc_ref[...] / l_ref[...]
    for g in range(B_GROUP):
        o = a[:, g * t:(g + 1) * t] - lam * a[:, (B_GROUP + g) * t:(B_GROUP + g + 1) * t]
        ms = jnp.mean(o * o, axis=0, keepdims=True)
        o = o * lax.rsqrt(ms + RMS_EPS) * sg_ref[...] * (1.0 - lam_init)
        o_ref[:, g * B_DK:(g + 1) * B_DK] = o.T.astype(BF16)


def _diff_prompt(qt, kb, vt, slopes, lam_p, subln_col, batch, seq, lam_init):
    t = ATT_T
    nq = seq // t
    nblk = 2 * B_GROUP
    width = B_GROUP * B_DK
    smem = pl.BlockSpec(memory_space=pltpu.SMEM)
    const = lambda a: pl.BlockSpec(a.shape, lambda b, j, i: (0, 0))
    o_spec = pl.BlockSpec((t, width), lambda b, j, i: (b * nq + i, j))
    return pl.pallas_call(
        functools.partial(_diff_prompt_kernel, lam_init=lam_init),
        out_shape=jax.ShapeDtypeStruct((batch * seq, B_HEADS * B_DK), BF16),
        grid=(batch, B_KV_HEADS, nq),
        in_specs=[
            smem,
            pl.BlockSpec((None, width, t), lambda b, j, i: (b * nq + i, j, 0)),
            pl.BlockSpec((seq, B_DK), lambda b, j, i: (b, j)),
            pl.BlockSpec((nq, B_DK, t), lambda b, j, i: (b, j, 0)),
            const(lam_p), const(subln_col),
        ],
        out_specs=o_spec,
        scratch_shapes=[
            pltpu.VMEM((B_DK, nblk * t), BF16),
            pltpu.VMEM((t, nblk * t), F32),
            pltpu.VMEM((t, nblk * t), F32),
            pltpu.VMEM((t, nblk * t), F32),
            pltpu.VMEM((t, nblk * t), F32),
            pltpu.VMEM((1, nblk * t), F32),
            pltpu.VMEM((1, nblk * t), F32),
            pltpu.VMEM((B_DK, nblk * t), F32),
        ],
        compiler_params=_params("parallel", "parallel", "arbitrary"),
        name="diff_prompt",
    )(slopes, qt, kb, vt, lam_p, subln_col)


def _diff_sample_kernel(pt_ref, q_ref, kn_ref, vn_ref, ck_hbm, cv_hbm, bias_ref, nbias_ref, slope_ref, t_ref,
                        lam_ref, sg_ref, o_ref, kring, vring, sem, m_ref, l_ref, acc_ref, *, past, lam_init):
    npg = PAGES_PER_STEP
    step = pl.program_id(1)
    steps_per_seq = pl.num_programs(1)
    n = pl.program_id(0) * steps_per_seq + step
    total = pl.num_programs(0) * steps_per_seq
    rows = 2 * B_HEADS * N_TOK

    def page_copies(stream_step):
        seq = stream_step // steps_per_seq
        first = (stream_step % steps_per_seq) * npg
        slot = stream_step % RING_SLOTS
        out = []
        for p in range(npg):
            page = pt_ref[seq, first + p]
            out.append(pltpu.make_async_copy(ck_hbm.at[page], kring.at[slot, p], sem.at[0, slot]))
            out.append(pltpu.make_async_copy(cv_hbm.at[page], vring.at[slot, p], sem.at[1, slot]))
        return out

    @pl.when(n == 0)
    def _():
        for ahead in range(RING_SLOTS - 1):
            for c in page_copies(ahead):
                c.start()

    @pl.when(n + (RING_SLOTS - 1) < total)
    def _():
        for c in page_copies(n + (RING_SLOTS - 1)):
            c.start()

    @pl.when(step == 0)
    def _():
        m_ref[...] = jnp.full(m_ref.shape, NEG, F32)
        l_ref[...] = jnp.zeros(l_ref.shape, F32)
        acc_ref[...] = jnp.zeros(acc_ref.shape, F32)

    q = q_ref[...]

    def update(ks, vs, bias, off):
        width = ks[0].shape[0]
        parts = [_dot_nt(q, kp.astype(BF16)) for kp in ks]
        s = (parts[0] if len(parts) == 1 else jnp.concatenate(parts, axis=1)) + bias
        m_old = m_ref[...]
        m_new = jnp.maximum(m_old, jnp.max(s, axis=-1, keepdims=True) + off)
        pe = jnp.exp(s - (m_new - off))
        alpha = jnp.exp(m_old - m_new)
        l_ref[...] = alpha * l_ref[...] + jnp.sum(pe, axis=-1, keepdims=True)
        pe = pe.astype(BF16)
        pv = None
        for idx, vp in enumerate(vs):
            term = _dot(pe[:, idx * width:(idx + 1) * width], vp.astype(BF16))
            pv = term if pv is None else pv + term
        acc_ref[...] = alpha * acc_ref[...] + pv
        m_ref[...] = m_new

    for c in page_copies(n):
        c.wait()
    slot = n % RING_SLOTS
    base = (step * (npg * PAGE_SIZE)).astype(F32)
    off = slope_ref[...] * (base - float(past) - t_ref[...])
    update([kring[slot, p] for p in range(npg)], [vring[slot, p] for p in range(npg)], bias_ref[...], off)

    @pl.when(step == steps_per_seq - 1)
    def _():
        pad = jnp.zeros((LANES - NEW_ROWS, B_DK), F32)
        knew = jnp.concatenate([kn_ref[...], pad], axis=0)
        vnew = jnp.concatenate([vn_ref[...], pad], axis=0)
        update([knew], [vnew], nbias_ref[...], 0.0)

        lam = _lambda(lam_ref, lam_init)
        a = acc_ref[...] / l_ref[...]
        half = rows // 2
        dm = a[0:half] - lam * a[half:rows]
        o_ref[...] = _rms(dm, sg_ref[...]) * (1.0 - lam_init)


def _diff_sample(page_table, qrows, knew, vnew, cache_k, cache_v, bias, nbias, slope_rows, t_rows,
                 lam_p, subln, lam_init):
    nseq, n_pages = page_table.shape
    npg = PAGES_PER_STEP
    rows = 2 * B_HEADS * N_TOK
    past = n_pages * PAGE_SIZE
    assert n_pages % npg == 0 and nseq * (n_pages // npg) >= RING_SLOTS - 1

    per_seq = lambda r: pl.BlockSpec((None, r, B_DK), lambda b, s, pt: (b, 0, 0))
    const = lambda a: pl.BlockSpec(a.shape, lambda b, s, pt: (0, 0))
    hbm = pl.BlockSpec(memory_space=pl.ANY)
    grid_spec = pltpu.PrefetchScalarGridSpec(
        num_scalar_prefetch=1,
        grid=(nseq, n_pages // npg),
        in_specs=[per_seq(rows), per_seq(NEW_ROWS), per_seq(NEW_ROWS), hbm, hbm,
                  const(bias), const(nbias), const(slope_rows), const(t_rows), const(lam_p), const(subln)],
        out_specs=per_seq(rows // 2),
        scratch_shapes=[
            pltpu.VMEM((RING_SLOTS, npg, PAGE_ROWS, B_DK), F32),
            pltpu.VMEM((RING_SLOTS, npg, PAGE_ROWS, B_DK), F32),
            pltpu.SemaphoreType.DMA((2, RING_SLOTS)),
            pltpu.VMEM((rows, 1), F32),
            pltpu.VMEM((rows, 1), F32),
            pltpu.VMEM((rows, B_DK), F32),
        ],
    )
    return pl.pallas_call(
        functools.partial(_diff_sample_kernel, past=past, lam_init=lam_init),
        out_shape=jax.ShapeDtypeStruct((nseq, rows // 2, B_DK), F32),
        grid_spec=grid_spec,
        compiler_params=_params("arbitrary", "arbitrary"),
        name="diff_sample",
    )(page_table, qrows, knew, vnew, cache_k, cache_v, bias, nbias, slope_rows, t_rows, lam_p, subln)


def _diff_sample_bias(slopes):
    rows = 2 * B_HEADS * N_TOK
    r = jnp.arange(rows)
    row_j = (r % (B_HEADS * N_TOK)) // (B_GROUP * N_TOK)
    row_t = r % N_TOK
    row_slope = slopes[(r % (B_HEADS * N_TOK)) // N_TOK]
    col = jnp.arange(PAGES_PER_STEP * PAGE_ROWS)
    same = row_j[:, None] == (col % B_KV_HEADS)[None, :]
    bias = jnp.where(same, row_slope[:, None] * (col // B_KV_HEADS).astype(F32)[None, :], NEG)
    ncol = jnp.arange(LANES)
    npos = ncol // B_KV_HEADS
    ok = ((row_j[:, None] == (ncol % B_KV_HEADS)[None, :]) & (npos[None, :] <= row_t[:, None])
          & (ncol < NEW_ROWS)[None, :])
    nbias = jnp.where(ok, row_slope[:, None] * (npos[None, :] - row_t[:, None]).astype(F32), NEG)
    return (bias.astype(F32), nbias.astype(F32), row_slope.reshape(rows, 1).astype(F32),
            row_t.reshape(rows, 1).astype(F32))


def _to_token_major(x):
    s, t, n = x.shape
    return jnp.transpose(x, (1, 0, 2)).reshape(t * s, n)


def _to_seq_major(x, seqs):
    n = x.shape[-1]
    return jnp.transpose(x.reshape(-1, seqs, n), (1, 0, 2))


def _pad_tokens(x, seqs):
    xs = _to_seq_major(x, seqs)
    return jnp.pad(xs, ((0, 0), (0, 8 - xs.shape[1]), (0, 0)))


def kernel(x_prompt, x_sample, cache_win_k, cache_win_v, cache_diff_k, cache_diff_v, page_table,
           c_prompt, c_sample, ada_w, ada_b, norm_g, w_qkv_a, w_o_a, sinks_a,
           w_qkv_b, w_o_b, lambda_b, subln_b, w_gu, w_down):
    batch, seq, d = x_prompt.shape
    nseq, ntok, _ = x_sample.shape
    depth = ada_w.shape[0]
    assert d == D_MODEL and ntok == N_TOK and depth == 2
    assert cache_diff_k.shape[0] == 1 and cache_win_k.shape[0] == 1
    n_pool = cache_diff_k.shape[1]
    prompt_tiles = seq // ROW_TILE
    sample_tiles = ntok

    mods = _adaln(jnp.concatenate([c_prompt, c_sample], axis=0), ada_w, ada_b)
    xp = x_prompt.reshape(batch * seq, d)
    xs = _to_token_major(x_sample)

    a_nk = A_KV_HEADS * HEAD_DIM
    b_nk = B_KV_HEADS * B_DK
    outs = {}
    wgu = w_gu.astype(BF16)
    wd = w_down.astype(BF16)

    layer = 0
    mod_p = mods[layer, :batch].reshape(batch, 1, N_MOD * d)
    mod_s = mods[layer, batch:].reshape(1, nseq, N_MOD * d)
    g = norm_g[layer]
    wqkv = w_qkv_a[0].astype(BF16)
    wo = w_o_a[0].astype(BF16)

    q, k, v = _qkv(xp, mod_p, g, wqkv, D_MODEL, a_nk, ROW_TILE, prompt_tiles)
    o = _win_prompt(q, k, v, sinks_a[0], batch, seq)
    xp = _out_ffn(o, xp, mod_p, g, wo, wgu, wd, layer, ROW_TILE, prompt_tiles)
    k4 = k.reshape(batch, seq, A_KV_HEADS, HEAD_DIM)
    v4 = v.reshape(batch, seq, A_KV_HEADS, HEAD_DIM)
    outs["wkp"] = k4[:, -WINDOW:][None]
    outs["wvp"] = v4[:, -WINDOW:][None]

    q, k, v = _qkv(xs, mod_s, g, wqkv, D_MODEL, a_nk, SAMPLE_TILE, sample_tiles)
    q5 = q.reshape(ntok, nseq, A_KV_HEADS, A_GROUP, HEAD_DIM)
    q5 = jnp.transpose(q5, (1, 2, 3, 0, 4))
    eye = jnp.eye(A_KV_HEADS, dtype=BF16)
    qrows = (q5[:, :, :, :, None, :] * eye[None, :, None, None, :, None]).reshape(nseq, A_HEADS * ntok, a_nk)
    kbuf = cache_win_k[0].reshape(nseq, WINDOW, a_nk)
    vbuf = cache_win_v[0].reshape(nseq, WINDOW, a_nk)
    knew = _pad_tokens(k, nseq)
    vnew = _pad_tokens(v, nseq)
    a_slopes = jnp.asarray([_a_slope(h) for h in range(A_HEADS)], F32)
    sink_rows = jnp.repeat(sinks_a[0].astype(F32), ntok).reshape(A_HEADS * ntok, 1)
    slope_rows = jnp.repeat(a_slopes, ntok).reshape(A_HEADS * ntok, 1)
    r16 = _win_sample(qrows, kbuf, knew, vbuf, vnew, sink_rows, slope_rows)
    o = r16.reshape(nseq, A_GROUP, ntok, A_KV_HEADS, HEAD_DIM)
    o = jnp.transpose(o, (2, 0, 3, 1, 4)).reshape(ntok * nseq, D_MODEL).astype(BF16)
    xs = _out_ffn(o, xs, mod_s, g, wo, wgu, wd, layer, SAMPLE_TILE, sample_tiles)
    outs["wks"] = jnp.concatenate([kbuf[:, ntok:], knew[:, :ntok]], axis=1).reshape(
        1, nseq, WINDOW, A_KV_HEADS, HEAD_DIM)
    outs["wvs"] = jnp.concatenate([vbuf[:, ntok:], vnew[:, :ntok]], axis=1).reshape(
        1, nseq, WINDOW, A_KV_HEADS, HEAD_DIM)

    layer = 1
    lam_init = 0.8 - 0.6 * math.exp(-0.3 * layer)
    mod_p = mods[layer, :batch].reshape(batch, 1, N_MOD * d)
    mod_s = mods[layer, batch:].reshape(1, nseq, N_MOD * d)
    g = norm_g[layer]
    wqkv = w_qkv_b[0].astype(BF16)
    wo = w_o_b[0].astype(BF16)
    lam_p = lambda_b[0].astype(F32)
    subln = subln_b[0].astype(F32)
    b_slopes = jnp.asarray([_b_slope(h) for h in range(B_HEADS)], F32)

    wkv = wqkv[:, D_MODEL:]
    wqt = wqkv[:, :D_MODEL].T
    wvt = wqkv[:, D_MODEL + b_nk:].T
    k, v, kb, qt, vt = _qkv_t(xp, mod_p, g, wkv, wqt, wvt, D_MODEL, b_nk, ROW_TILE, prompt_tiles)
    o = _diff_prompt(qt, kb, vt, b_slopes, lam_p, subln.reshape(B_DK, 1), batch, seq, lam_init)
    xp = _out_ffn(o, xp, mod_p, g, wo, wgu, wd, layer, ROW_TILE, prompt_tiles)
    outs["dkp"] = k.reshape(1, batch, seq, B_KV_HEADS, B_DK)
    outs["dvp"] = v.reshape(1, batch, seq, B_KV_HEADS, B_DK)

    q, k, v = _qkv(xs, mod_s, g, wqkv, D_MODEL, b_nk, SAMPLE_TILE, sample_tiles)
    q6 = q.reshape(ntok, nseq, B_KV_HEADS, B_GROUP, 2, HEAD_DIM)
    q6 = jnp.transpose(q6, (1, 4, 2, 3, 0, 5))
    eye_m = jnp.eye(2, dtype=BF16)
    qrows = (q6[:, :, :, :, :, None, :] * eye_m[None, :, None, None, None, :, None]).reshape(
        nseq, 2 * B_HEADS * ntok, B_DK)
    ks = _to_seq_major(k, nseq)
    vs = _to_seq_major(v, nseq)
    bias, nbias, slope_rows, t_rows = _diff_sample_bias(b_slopes)
    ck = cache_diff_k.reshape(n_pool, PAGE_ROWS, B_DK)
    cv = cache_diff_v.reshape(n_pool, PAGE_ROWS, B_DK)
    r32 = _diff_sample(page_table, qrows, ks.reshape(nseq, NEW_ROWS, B_DK), vs.reshape(nseq, NEW_ROWS, B_DK),
                       ck, cv, bias, nbias, slope_rows, t_rows, lam_p, subln.reshape(1, B_DK), lam_init)
    o = r32.reshape(nseq, B_KV_HEADS, B_GROUP, ntok, B_DK)
    o = jnp.transpose(o, (3, 0, 1, 2, 4)).reshape(ntok * nseq, D_MODEL).astype(BF16)
    xs = _out_ffn(o, xs, mod_s, g, wo, wgu, wd, layer, SAMPLE_TILE, sample_tiles)
    outs["dks"] = ks.reshape(1, nseq, ntok, B_KV_HEADS, B_DK)
    outs["dvs"] = vs.reshape(1, nseq, ntok, B_KV_HEADS, B_DK)

    y_prompt = xp.reshape(batch, seq, d)
    y_sample = _to_seq_major(xs, nseq)
    return (y_prompt, y_sample, outs["wkp"], outs["wvp"], outs["wks"], outs["wvs"],
            outs["dkp"], outs["dvp"], outs["dks"], outs["dvs"])
```

```python
import functools
import math

import jax
import jax.numpy as jnp
from jax import lax
from jax.experimental import pallas as pl
from jax.experimental.pallas import tpu as pltpu

F32 = jnp.float32
BF16 = jnp.bfloat16

D_MODEL = 1024
HEAD_DIM = 64
WINDOW = 128
A_HEADS = 16
A_KV_HEADS = 4
A_GROUP = 4
B_HEADS = 8
B_KV_HEADS = 4
B_GROUP = 2
B_DK = 128
D_FF = 2816
PAGE_SIZE = 128
RMS_EPS = 1e-6
N_MOD = 6
N_TOK = 4

LANES = 128
MXU_WIDTH = 256
VMEM_LIMIT = 56 * 1024 * 1024
NEG = -1e30
LOG2E = 1.4426950408889634
BIG = 1e30

ROW_TILE = 512
SAMPLE_TILE = 128
FF_CHUNK = MXU_WIDTH
ATT_T = 256
PAGES_PER_STEP = 16
RING_SLOTS = 3
PAGE_ROWS = PAGE_SIZE * B_KV_HEADS
NEW_ROWS = N_TOK * B_KV_HEADS
WIN_SAMPLE_SEQS = 8


def _dot(a, b):
    return jnp.dot(a, b, preferred_element_type=F32)


def _dot_nt(a, b):
    return lax.dot_general(a, b, (((1,), (1,)), ((), ())), preferred_element_type=F32)


def _rms(x, g):
    ms = jnp.mean(x * x, axis=-1, keepdims=True)
    return x * lax.rsqrt(ms + RMS_EPS) * g


def _silu(x):
    return x / (1.0 + jnp.exp(-x))


def _params(*sem):
    return pltpu.CompilerParams(dimension_semantics=sem, vmem_limit_bytes=VMEM_LIMIT)


def _adaln_kernel(c_ref, w_ref, b_ref, o_ref):
    s = _silu(c_ref[...]).astype(BF16)
    o_ref[0] = _dot(s, w_ref[0].astype(BF16)) + b_ref[0]


def _adaln(c_all, ada_w, ada_b):
    depth, d, n = ada_w.shape
    rows = c_all.shape[0]
    tn = D_MODEL
    return pl.pallas_call(
        _adaln_kernel,
        out_shape=jax.ShapeDtypeStruct((depth, rows, n), F32),
        grid=(depth, n // tn),
        in_specs=[
            pl.BlockSpec((rows, d), lambda l, t: (0, 0)),
            pl.BlockSpec((1, d, tn), lambda l, t: (l, 0, t)),
            pl.BlockSpec((1, 1, tn), lambda l, t: (l, 0, t)),
        ],
        out_specs=pl.BlockSpec((1, rows, tn), lambda l, t: (l, 0, t)),
        compiler_params=_params("parallel", "parallel"),
        name="adaln",
    )(c_all, ada_w, ada_b.reshape(depth, 1, n))


def _prenorm(x_ref, mod_ref, g_ref):
    shift = mod_ref[0, :, 0:D_MODEL]
    scale = mod_ref[0, :, D_MODEL:2 * D_MODEL]
    return (_rms(x_ref[...], g_ref[0:1, :]) * (1.0 + scale) + shift).astype(BF16)


def _qkv_kernel(x_ref, mod_ref, g_ref, w_ref, q_ref, k_ref, v_ref, *, nq, nk):
    qkv = _dot(_prenorm(x_ref, mod_ref, g_ref), w_ref[...])
    q_ref[...] = (qkv[:, :nq] * HEAD_DIM ** -0.5).astype(BF16)
    k_ref[...] = qkv[:, nq:nq + nk]
    v_ref[...] = qkv[:, nq + nk:]


def _qkv(x, mod, g, w, nq, nk, tm, tiles_per_group):
    rows = x.shape[0]
    r = mod.shape[1]
    row_spec = lambda n: pl.BlockSpec((tm, n), lambda i: (i, 0))
    return pl.pallas_call(
        functools.partial(_qkv_kernel, nq=nq, nk=nk),
        out_shape=(
            jax.ShapeDtypeStruct((rows, nq), BF16),
            jax.ShapeDtypeStruct((rows, nk), F32),
            jax.ShapeDtypeStruct((rows, nk), F32),
        ),
        grid=(rows // tm,),
        in_specs=[
            row_spec(D_MODEL),
            pl.BlockSpec((1, r, N_MOD * D_MODEL), lambda i: (i // tiles_per_group, 0, 0)),
            pl.BlockSpec(g.shape, lambda i: (0, 0)),
            pl.BlockSpec(w.shape, lambda i: (0, 0)),
        ],
        out_specs=(row_spec(nq), row_spec(nk), row_spec(nk)),
        compiler_params=_params("parallel"),
        name="prenorm_qkv",
    )(x, mod, g, w)


def _qkv_t_kernel(x_ref, mod_ref, g_ref, wkv_ref, wqt_ref, wvt_ref,
                  k_ref, v_ref, kb_ref, qt_ref, vt_ref, *, nk):
    h = _prenorm(x_ref, mod_ref, g_ref)
    kv = _dot(h, wkv_ref[...])
    k = kv[:, :nk]
    tm = k.shape[0]
    for jj in range(B_KV_HEADS):
        k_ref[pl.ds(jj, tm, stride=B_KV_HEADS), :] = kv[:, jj * B_DK:(jj + 1) * B_DK]
        v_ref[pl.ds(jj, tm, stride=B_KV_HEADS), :] = kv[:, nk + jj * B_DK:nk + (jj + 1) * B_DK]
    kb_ref[...] = k.astype(BF16)
    qt = (_dot_nt(wqt_ref[...], h) * (HEAD_DIM ** -0.5 * LOG2E)).astype(BF16)
    vt = _dot_nt(wvt_ref[...], h).astype(BF16)
    for c in range(qt_ref.shape[0]):
        qt_ref[c] = qt[:, c * ATT_T:(c + 1) * ATT_T]
        vt_ref[c] = vt[:, c * ATT_T:(c + 1) * ATT_T]


def _qkv_t(x, mod, g, wkv, wqt, wvt, nq, nk, tm, tiles_per_group):
    rows = x.shape[0]
    r = mod.shape[1]
    per = tm // ATT_T
    row_spec = lambda n: pl.BlockSpec((tm, n), lambda i: (i, 0))
    const = lambda a: pl.BlockSpec(a.shape, lambda i: (0, 0))
    return pl.pallas_call(
        functools.partial(_qkv_t_kernel, nk=nk),
        out_shape=(
            jax.ShapeDtypeStruct((rows * B_KV_HEADS, B_DK), F32),
            jax.ShapeDtypeStruct((rows * B_KV_HEADS, B_DK), F32),
            jax.ShapeDtypeStruct((rows, nk), BF16),
            jax.ShapeDtypeStruct((rows // ATT_T, nq, ATT_T), BF16),
            jax.ShapeDtypeStruct((rows // ATT_T, nk, ATT_T), BF16),
        ),
        grid=(rows // tm,),
        in_specs=[
            row_spec(D_MODEL),
            pl.BlockSpec((1, r, N_MOD * D_MODEL), lambda i: (i // tiles_per_group, 0, 0)),
            const(g), const(wkv), const(wqt), const(wvt),
        ],
        out_specs=(pl.BlockSpec((tm * B_KV_HEADS, B_DK), lambda i: (i, 0)),
                   pl.BlockSpec((tm * B_KV_HEADS, B_DK), lambda i: (i, 0)),
                   row_spec(nk),
                   pl.BlockSpec((per, nq, ATT_T), lambda i: (i, 0, 0)),
                   pl.BlockSpec((per, nk, ATT_T), lambda i: (i, 0, 0))),
        compiler_params=_params("parallel"),
        name="prenorm_qkv_t",
    )(x, mod, g, wkv, wqt, wvt)


def _out_ffn_kernel(o_ref, x_ref, mod_ref, g_ref, wo_ref, wgu_ref, wd_ref, y_ref):
    gate1 = mod_ref[0, :, 2 * D_MODEL:3 * D_MODEL]
    shift2 = mod_ref[0, :, 3 * D_MODEL:4 * D_MODEL]
    scale2 = mod_ref[0, :, 4 * D_MODEL:5 * D_MODEL]
    gate2 = mod_ref[0, :, 5 * D_MODEL:6 * D_MODEL]
    y = _dot(o_ref[...], wo_ref[...])
    x1 = x_ref[...] + gate1 * _rms(y, g_ref[1:2, :])
    h = (_rms(x1, g_ref[2:3, :]) * (1.0 + scale2) + shift2).astype(BF16)
    acc = jnp.zeros(x1.shape, F32)
    for c in range(D_FF // FF_CHUNK):
        lo = c * FF_CHUNK
        gch = _dot(h, wgu_ref[:, lo:lo + FF_CHUNK])
        uch = _dot(h, wgu_ref[:, D_FF + lo:D_FF + lo + FF_CHUNK])
        a = (_silu(gch) * uch).astype(BF16)
        acc = acc + _dot(a, wd_ref[lo:lo + FF_CHUNK, :])
    y_ref[...] = x1 + gate2 * _rms(acc, g_ref[3:4, :])


def _out_ffn(o, x, mod, g, wo, wgu, wd, layer, tm, tiles_per_group):
    rows = x.shape[0]
    r = mod.shape[1]
    row_spec = pl.BlockSpec((tm, D_MODEL), lambda i: (i, 0))
    const = lambda a: pl.BlockSpec(a.shape, lambda i: (0, 0), pipeline_mode=pl.Buffered(1))
    of_layer = lambda a: pl.BlockSpec((None,) + a.shape[1:], lambda i: (layer, 0, 0),
                                      pipeline_mode=pl.Buffered(1))
    return pl.pallas_call(
        _out_ffn_kernel,
        out_shape=jax.ShapeDtypeStruct((rows, D_MODEL), F32),
        grid=(rows // tm,),
        in_specs=[
            row_spec,
            row_spec,
            pl.BlockSpec((1, r, N_MOD * D_MODEL), lambda i: (i // tiles_per_group, 0, 0)),
            const(g), const(wo), of_layer(wgu), of_layer(wd),
        ],
        out_specs=row_spec,
        compiler_params=_params("parallel"),
        name="out_ffn",
    )(o, x, mod, g, wo, wgu, wd)


def _a_slope(h):
    return 2.0 ** (-8.0 * (h + 1) / A_HEADS)


def _b_slope(h):
    return 2.0 ** (-8.0 * (h + 1) / B_HEADS)


def _half_lane_pair(x, kv_head, lo):
    rolled = pltpu.roll(x, HEAD_DIM, 1)
    first, second = (x, rolled) if kv_head % 2 == 0 else (rolled, x)
    return (jnp.where(lo, first, 0.0).astype(BF16), jnp.where(lo, 0.0, second).astype(BF16))


def _win_prompt_kernel(sink_ref, q_ref, kc_ref, kp_ref, vc_ref, vp_ref, o_ref):
    i = pl.program_id(1)
    kk = jnp.concatenate([kp_ref[...], kc_ref[...]], axis=0)
    vv = jnp.concatenate([vp_ref[...], vc_ref[...]], axis=0)
    row = lax.broadcasted_iota(jnp.int32, (WINDOW, 2 * WINDOW), 0)
    col = lax.broadcasted_iota(jnp.int32, (WINDOW, 2 * WINDOW), 1)
    rel = row - col + WINDOW
    valid = (rel >= 0) & (rel < WINDOW) & ((col >= WINDOW) | (i > 0))
    relm = jnp.where(valid, rel.astype(F32), BIG)
    lo = lax.broadcasted_iota(jnp.int32, (2 * WINDOW, LANES), 1) < HEAD_DIM
    for j in range(A_KV_HEADS):
        sl = slice((j // 2) * LANES, (j // 2 + 1) * LANES)
        k_pair = _half_lane_pair(kk[:, sl], j, lo)
        v_pair = _half_lane_pair(vv[:, sl], j, lo)
        for p in range(A_GROUP // 2):
            qsl = slice((2 * j + p) * LANES, (2 * j + p + 1) * LANES)
            qs = q_ref[:, qsl]
            out = None
            for half in range(2):
                h = A_GROUP * j + 2 * p + half
                s = _dot_nt(qs, k_pair[half]) - _a_slope(h) * relm
                sink = sink_ref[h]
                m = jnp.maximum(jnp.max(s, axis=-1, keepdims=True), sink)
                pe = jnp.exp(s - m)
                l = jnp.sum(pe, axis=-1, keepdims=True) + jnp.exp(sink - m)
                o = _dot(pe.astype(BF16), v_pair[half]) / l
                out = o if out is None else out + o
            o_ref[:, qsl] = out.astype(BF16)


def _win_prompt(q, k, v, sinks, batch, seq):
    nb = seq // WINDOW
    nk = A_KV_HEADS * HEAD_DIM
    cur = lambda b, i: (b * nb + i, 0)
    prev = lambda b, i: (b * nb + jnp.maximum(i - 1, 0), 0)
    smem = pl.BlockSpec(memory_space=pltpu.SMEM)
    return pl.pallas_call(
        _win_prompt_kernel,
        out_shape=jax.ShapeDtypeStruct(q.shape, BF16),
        grid=(batch, nb),
        in_specs=[
            smem,
            pl.BlockSpec((WINDOW, D_MODEL), cur),
            pl.BlockSpec((WINDOW, nk), cur),
            pl.BlockSpec((WINDOW, nk), prev),
            pl.BlockSpec((WINDOW, nk), cur),
            pl.BlockSpec((WINDOW, nk), prev),
        ],
        out_specs=pl.BlockSpec((WINDOW, D_MODEL), cur),
        compiler_params=_params("parallel", "parallel"),
        name="win_prompt",
    )(sinks, q, k, k, v, v)


def _win_sample_kernel(q_ref, kbuf_ref, knew_ref, vbuf_ref, vnew_ref, sink_ref, slope_ref, o_ref, wk_ref, wv_ref):
    rows = A_HEADS * N_TOK
    keys = 2 * WINDOW
    r = lax.broadcasted_iota(jnp.int32, (rows, keys), 0)
    pos = lax.broadcasted_iota(jnp.int32, (rows, keys), 1)
    rel = (r % N_TOK) + WINDOW - pos
    valid = (rel >= 0) & (rel < WINDOW)
    bias = jnp.where(valid, -slope_ref[...] * rel.astype(F32), NEG)
    nk = A_KV_HEADS * HEAD_DIM
    own = (lax.broadcasted_iota(jnp.int32, (rows, nk), 1) // HEAD_DIM
           == lax.broadcasted_iota(jnp.int32, (rows, nk), 0) // (A_GROUP * N_TOK))
    sink = sink_ref[...]
    pad = jnp.zeros((keys - WINDOW - 8, nk), F32)
    for b in range(WIN_SAMPLE_SEQS):
        kall = jnp.concatenate([kbuf_ref[b], knew_ref[b], pad], axis=0).astype(BF16)
        vall = jnp.concatenate([vbuf_ref[b], vnew_ref[b], pad], axis=0).astype(BF16)
        s = _dot_nt(q_ref[b], kall) + bias
        m = jnp.maximum(jnp.max(s, axis=-1, keepdims=True), sink)
        pe = jnp.exp(s - m)
        l = jnp.sum(pe, axis=-1, keepdims=True) + jnp.exp(sink - m)
        o = jnp.where(own, _dot(pe.astype(BF16), vall) / l, 0.0)
        o_ref[b] = o[0:16] + o[16:32] + o[32:48] + o[48:64]
        for buf_ref, new_ref, win_ref in ((kbuf_ref, knew_ref, wk_ref), (vbuf_ref, vnew_ref, wv_ref)):
            win_ref[b, 0:WINDOW - N_TOK, :] = buf_ref[b, N_TOK:WINDOW, :]
            win_ref[b, WINDOW - N_TOK:WINDOW, :] = new_ref[b, 0:N_TOK, :]


def _win_sample(qrows, kbuf, knew, vbuf, vnew, sink_rows, slope_rows):
    nseq = qrows.shape[0]
    sb = WIN_SAMPLE_SEQS
    nk = A_KV_HEADS * HEAD_DIM
    rows = A_HEADS * N_TOK
    blk = lambda r, c: pl.BlockSpec((sb, r, c), lambda i: (i, 0, 0))
    col = pl.BlockSpec((rows, 1), lambda i: (0, 0))
    return pl.pallas_call(
        _win_sample_kernel,
        out_shape=(jax.ShapeDtypeStruct((nseq, A_GROUP * N_TOK, nk), F32),
                   jax.ShapeDtypeStruct((nseq, WINDOW, nk), F32),
                   jax.ShapeDtypeStruct((nseq, WINDOW, nk), F32)),
        grid=(nseq // sb,),
        in_specs=[blk(rows, nk), blk(WINDOW, nk), blk(8, nk), blk(WINDOW, nk), blk(8, nk), col, col],
        out_specs=(blk(A_GROUP * N_TOK, nk), blk(WINDOW, nk), blk(WINDOW, nk)),
        compiler_params=_params("parallel"),
        name="win_sample",
    )(qrows, kbuf, knew, vbuf, vnew, sink_rows, slope_rows)


def _lambda(lam_ref, lam_init):
    lp = lam_ref[...]
    e1 = jnp.exp(jnp.sum(lp[0:1] * lp[1:2], axis=-1, keepdims=True))
    e2 = jnp.exp(jnp.sum(lp[2:3] * lp[3:4], axis=-1, keepdims=True))
    return e1 - e2 + lam_init


def _diff_prompt_kernel(slope_ref, qt_ref, k_ref, vt_ref, lam_ref, sg_ref, o_ref,
                        qs_ref, bias_ref, dbias_ref, sa_ref, sb_ref, m_ref, l_ref, acc_ref, *, lam_init):
    t = ATT_T
    nblk = 2 * B_GROUP
    w = nblk * t
    j = pl.program_id(1)
    i = pl.program_id(2)
    first_map = lax.broadcasted_iota(jnp.int32, (B_DK, t), 0) < HEAD_DIM
    zero = jnp.zeros((B_DK, t), BF16)
    for g in range(B_GROUP):
        qg = qt_ref[g * B_DK:(g + 1) * B_DK, :]
        qs_ref[:, g * t:(g + 1) * t] = jnp.where(first_map, qg, zero)
        qs_ref[:, (B_GROUP + g) * t:(B_GROUP + g + 1) * t] = jnp.where(first_map, zero, qg)

    @pl.when(i == 0)
    def _():
        key = lax.broadcasted_iota(jnp.int32, (t, t), 0)
        qry = lax.broadcasted_iota(jnp.int32, (t, t), 1)
        dist = (key - qry).astype(F32)
        for g in range(B_GROUP):
            bg = (slope_ref[B_GROUP * j + g] * LOG2E) * dist
            dg = jnp.where(key <= qry, bg, NEG)
            for blk in (g, B_GROUP + g):
                bias_ref[:, blk * t:(blk + 1) * t] = bg
                dbias_ref[:, blk * t:(blk + 1) * t] = dg

    m_ref[...] = jnp.full(m_ref.shape, NEG, F32)
    l_ref[...] = jnp.zeros(l_ref.shape, F32)
    acc_ref[...] = jnp.zeros(acc_ref.shape, F32)
    colg = (lax.broadcasted_iota(jnp.int32, (1, w), 1) // t) % B_GROUP
    srow = jnp.where(colg == 0, slope_ref[B_GROUP * j], slope_ref[B_GROUP * j + 1]) * LOG2E

    def scores(c):
        start = pl.multiple_of(c * t, t)
        return _dot(k_ref[pl.ds(start, t), :], qs_ref[...])

    def update(s, c):
        off = srow * ((c - i) * t).astype(F32)
        m_old = m_ref[...]
        m_new = jnp.maximum(m_old, jnp.max(s, axis=0, keepdims=True) + off)
        pe = jnp.exp2(s - (m_new - off))
        alpha = jnp.exp2(m_old - m_new)
        l_ref[...] = alpha * l_ref[...] + jnp.sum(pe, axis=0, keepdims=True)
        acc_ref[...] = alpha * acc_ref[...] + _dot(vt_ref[c], pe.astype(BF16))
        m_ref[...] = m_new

    sa_ref[...] = scores(0)

    def pair(p, carry):
        c0 = 2 * p
        sb_ref[...] = scores(c0 + 1)
        update(sa_ref[...] + bias_ref[...], c0)
        sa_ref[...] = scores(c0 + 2)
        update(sb_ref[...] + bias_ref[...], c0 + 1)
        return carry

    lax.fori_loop(0, i // 2, pair, 0)

    @pl.when(i % 2 == 1)
    def _():
        sb_ref[...] = scores(i)
        update(sa_ref[...] + bias_ref[...], i - 1)
        update(sb_ref[...] + dbias_ref[...], i)

    @pl.when(i % 2 == 0)
    def _():
        update(sa_ref[...] + dbias_ref[...], i)

    lam = _lambda(lam_ref, lam_init)
    a = acc_ref[...] / l_ref[...]
    for g in range(B_GROUP):
        o = a[:, g * t:(g + 1) * t] - lam * a[:, (B_GROUP + g) * t:(B_GROUP + g + 1) * t]
        ms = jnp.mean(o * o, axis=0, keepdims=True)
        o = o * lax.rsqrt(ms + RMS_EPS) * sg_ref[...] * (1.0 - lam_init)
        o_ref[:, g * B_DK:(g + 1) * B_DK] = o.T.astype(BF16)


def _diff_prompt(qt, kb, vt, slopes, lam_p, subln_col, batch, seq, lam_init):
    t = ATT_T
    nq = seq // t
    nblk = 2 * B_GROUP
    width = B_GROUP * B_DK
    smem = pl.BlockSpec(memory_space=pltpu.SMEM)
    const = lambda a: pl.BlockSpec(a.shape, lambda b, j, i: (0, 0))
    o_spec = pl.BlockSpec((t, width), lambda b, j, i: (b * nq + i, j))
    return pl.pallas_call(
        functools.partial(_diff_prompt_kernel, lam_init=lam_init),
        out_shape=jax.ShapeDtypeStruct((batch * seq, B_HEADS * B_DK), BF16),
        grid=(batch, B_KV_HEADS, nq),
        in_specs=[
            smem,
            pl.BlockSpec((None, width, t), lambda b, j, i: (b * nq + i, j, 0)),
            pl.BlockSpec((seq, B_DK), lambda b, j, i: (b, j)),
            pl.BlockSpec((nq, B_DK, t), lambda b, j, i: (b, j, 0)),
            const(lam_p), const(subln_col),
        ],
        out_specs=o_spec,
        scratch_shapes=[
            pltpu.VMEM((B_DK, nblk * t), BF16),
            pltpu.VMEM((t, nblk * t), F32),
            pltpu.VMEM((t, nblk * t), F32),
            pltpu.VMEM((t, nblk * t), F32),
            pltpu.VMEM((t, nblk * t), F32),
            pltpu.VMEM((1, nblk * t), F32),
            pltpu.VMEM((1, nblk * t), F32),
            pltpu.VMEM((B_DK, nblk * t), F32),
        ],
        compiler_params=_params("parallel", "parallel", "arbitrary"),
        name="diff_prompt",
    )(slopes, qt, kb, vt, lam_p, subln_col)


def _diff_sample_kernel(pt_ref, q_ref, kn_ref, vn_ref, ck_hbm, cv_hbm, bias_ref, nbias_ref, slope_ref, t_ref,
                        lam_ref, sg_ref, o_ref, kring, vring, sem, m_ref, l_ref, acc_ref, *, past, lam_init):
    npg = PAGES_PER_STEP
    step = pl.program_id(1)
    steps_per_seq = pl.num_programs(1)
    n = pl.program_id(0) * steps_per_seq + step
    total = pl.num_programs(0) * steps_per_seq
    rows = 2 * B_HEADS * N_TOK

    def page_copies(stream_step):
        seq = stream_step // steps_per_seq
        first = (stream_step % steps_per_seq) * npg
        slot = stream_step % RING_SLOTS
        out = []
        for p in range(npg):
            page = pt_ref[seq, first + p]
            out.append(pltpu.make_async_copy(ck_hbm.at[page], kring.at[slot, p], sem.at[0, slot]))
            out.append(pltpu.make_async_copy(cv_hbm.at[page], vring.at[slot, p], sem.at[1, slot]))
        return out

    @pl.when(n == 0)
    def _():
        for ahead in range(RING_SLOTS - 1):
            for c in page_copies(ahead):
                c.start()

    @pl.when(n + (RING_SLOTS - 1) < total)
    def _():
        for c in page_copies(n + (RING_SLOTS - 1)):
            c.start()

    @pl.when(step == 0)
    def _():
        m_ref[...] = jnp.full(m_ref.shape, NEG, F32)
        l_ref[...] = jnp.zeros(l_ref.shape, F32)
        acc_ref[...] = jnp.zeros(acc_ref.shape, F32)

    q = q_ref[...]

    def update(ks, vs, bias, off):
        width = ks[0].shape[0]
        parts = [_dot_nt(q, kp.astype(BF16)) for kp in ks]
        s = (parts[0] if len(parts) == 1 else jnp.concatenate(parts, axis=1)) + bias
        m_old = m_ref[...]
        m_new = jnp.maximum(m_old, jnp.max(s, axis=-1, keepdims=True) + off)
        pe = jnp.exp(s - (m_new - off))
        alpha = jnp.exp(m_old - m_new)
        l_ref[...] = alpha * l_ref[...] + jnp.sum(pe, axis=-1, keepdims=True)
        pe = pe.astype(BF16)
        pv = None
        for idx, vp in enumerate(vs):
            term = _dot(pe[:, idx * width:(idx + 1) * width], vp.astype(BF16))
            pv = term if pv is None else pv + term
        acc_ref[...] = alpha * acc_ref[...] + pv
        m_ref[...] = m_new

    for c in page_copies(n):
        c.wait()
    slot = n % RING_SLOTS
    base = (step * (npg * PAGE_SIZE)).astype(F32)
    off = slope_ref[...] * (base - float(past) - t_ref[...])
    update([kring[slot, p] for p in range(npg)], [vring[slot, p] for p in range(npg)], bias_ref[...], off)

    @pl.when(step == steps_per_seq - 1)
    def _():
        pad = jnp.zeros((LANES - NEW_ROWS, B_DK), F32)
        knew = jnp.concatenate([kn_ref[...], pad], axis=0)
        vnew = jnp.concatenate([vn_ref[...], pad], axis=0)
        update([knew], [vnew], nbias_ref[...], 0.0)

        lam = _lambda(lam_ref, lam_init)
        a = acc_ref[...] / l_ref[...]
        half = rows // 2
        dm = a[0:half] - lam * a[half:rows]
        o_ref[...] = _rms(dm, sg_ref[...]) * (1.0 - lam_init)


def _diff_sample(page_table, qrows, knew, vnew, cache_k, cache_v, bias, nbias, slope_rows, t_rows,
                 lam_p, subln, lam_init):
    nseq, n_pages = page_table.shape
    npg = PAGES_PER_STEP
    rows = 2 * B_HEADS * N_TOK
    past = n_pages * PAGE_SIZE
    assert n_pages % npg == 0 and nseq * (n_pages // npg) >= RING_SLOTS - 1

    per_seq = lambda r: pl.BlockSpec((None, r, B_DK), lambda b, s, pt: (b, 0, 0))
    const = lambda a: pl.BlockSpec(a.shape, lambda b, s, pt: (0, 0))
    hbm = pl.BlockSpec(memory_space=pl.ANY)
    grid_spec = pltpu.PrefetchScalarGridSpec(
        num_scalar_prefetch=1,
        grid=(nseq, n_pages // npg),
        in_specs=[per_seq(rows), per_seq(NEW_ROWS), per_seq(NEW_ROWS), hbm, hbm,
                  const(bias), const(nbias), const(slope_rows), const(t_rows), const(lam_p), const(subln)],
        out_specs=per_seq(rows // 2),
        scratch_shapes=[
            pltpu.VMEM((RING_SLOTS, npg, PAGE_ROWS, B_DK), F32),
            pltpu.VMEM((RING_SLOTS, npg, PAGE_ROWS, B_DK), F32),
            pltpu.SemaphoreType.DMA((2, RING_SLOTS)),
            pltpu.VMEM((rows, 1), F32),
            pltpu.VMEM((rows, 1), F32),
            pltpu.VMEM((rows, B_DK), F32),
        ],
    )
    return pl.pallas_call(
        functools.partial(_diff_sample_kernel, past=past, lam_init=lam_init),
        out_shape=jax.ShapeDtypeStruct((nseq, rows // 2, B_DK), F32),
        grid_spec=grid_spec,
        compiler_params=_params("arbitrary", "arbitrary"),
        name="diff_sample",
    )(page_table, qrows, knew, vnew, cache_k, cache_v, bias, nbias, slope_rows, t_rows, lam_p, subln)


def _diff_sample_bias(slopes):
    rows = 2 * B_HEADS * N_TOK
    r = jnp.arange(rows)
    row_j = (r % (B_HEADS * N_TOK)) // (B_GROUP * N_TOK)
    row_t = r % N_TOK
    row_slope = slopes[(r % (B_HEADS * N_TOK)) // N_TOK]
    col = jnp.arange(PAGES_PER_STEP * PAGE_ROWS)
    same = row_j[:, None] == (col % B_KV_HEADS)[None, :]
    bias = jnp.where(same, row_slope[:, None] * (col // B_KV_HEADS).astype(F32)[None, :], NEG)
    ncol = jnp.arange(LANES)
    npos = ncol // B_KV_HEADS
    ok = ((row_j[:, None] == (ncol % B_KV_HEADS)[None, :]) & (npos[None, :] <= row_t[:, None])
          & (ncol < NEW_ROWS)[None, :])
    nbias = jnp.where(ok, row_slope[:, None] * (npos[None, :] - row_t[:, None]).astype(F32), NEG)
    return (bias.astype(F32), nbias.astype(F32), row_slope.reshape(rows, 1).astype(F32),
            row_t.reshape(rows, 1).astype(F32))


def _to_token_major(x):
    s, t, n = x.shape
    return jnp.transpose(x, (1, 0, 2)).reshape(t * s, n)


def _to_seq_major(x, seqs):
    n = x.shape[-1]
    return jnp.transpose(x.reshape(-1, seqs, n), (1, 0, 2))


def _pad_tokens(x, seqs):
    xs = _to_seq_major(x, seqs)
    return jnp.pad(xs, ((0, 0), (0, 8 - xs.shape[1]), (0, 0)))


def kernel(x_prompt, x_sample, cache_win_k, cache_win_v, cache_diff_k, cache_diff_v, page_table,
           c_prompt, c_sample, ada_w, ada_b, norm_g, w_qkv_a, w_o_a, sinks_a,
           w_qkv_b, w_o_b, lambda_b, subln_b, w_gu, w_down):
    batch, seq, d = x_prompt.shape
    nseq, ntok, _ = x_sample.shape
    depth = ada_w.shape[0]
    assert d == D_MODEL and ntok == N_TOK and depth == 2
    assert cache_diff_k.shape[0] == 1 and cache_win_k.shape[0] == 1
    n_pool = cache_diff_k.shape[1]
    prompt_tiles = seq // ROW_TILE
    sample_tiles = ntok

    mods = _adaln(jnp.concatenate([c_prompt, c_sample], axis=0), ada_w, ada_b)
    xp = x_prompt.reshape(batch * seq, d)
    xs = _to_token_major(x_sample)

    a_nk = A_KV_HEADS * HEAD_DIM
    b_nk = B_KV_HEADS * B_DK
    outs = {}
    wgu = w_gu.astype(BF16)
    wd = w_down.astype(BF16)

    layer = 0
    mod_p = mods[layer, :batch].reshape(batch, 1, N_MOD * d)
    mod_s = mods[layer, batch:].reshape(1, nseq, N_MOD * d)
    g = norm_g[layer]
    wqkv = w_qkv_a[0].astype(BF16)
    wo = w_o_a[0].astype(BF16)

    q, k, v = _qkv(xp, mod_p, g, wqkv, D_MODEL, a_nk, ROW_TILE, prompt_tiles)
    o = _win_prompt(q, k, v, sinks_a[0], batch, seq)
    xp = _out_ffn(o, xp, mod_p, g, wo, wgu, wd, layer, ROW_TILE, prompt_tiles)
    k4 = k.reshape(batch, seq, A_KV_HEADS, HEAD_DIM)
    v4 = v.reshape(batch, seq, A_KV_HEADS, HEAD_DIM)
    outs["wkp"] = k4[:, -WINDOW:][None]
    outs["wvp"] = v4[:, -WINDOW:][None]

    q, k, v = _qkv(xs, mod_s, g, wqkv, D_MODEL, a_nk, SAMPLE_TILE, sample_tiles)
    q5 = q.reshape(ntok, nseq, A_KV_HEADS, A_GROUP, HEAD_DIM)
    q5 = jnp.transpose(q5, (1, 2, 3, 0, 4))
    eye = jnp.eye(A_KV_HEADS, dtype=BF16)
    qrows = (q5[:, :, :, :, None, :] * eye[None, :, None, None, :, None]).reshape(nseq, A_HEADS * ntok, a_nk)
    kbuf = cache_win_k[0].reshape(nseq, WINDOW, a_nk)
    vbuf = cache_win_v[0].reshape(nseq, WINDOW, a_nk)
    knew = _pad_tokens(k, nseq)
    vnew = _pad_tokens(v, nseq)
    a_slopes = jnp.asarray([_a_slope(h) for h in range(A_HEADS)], F32)
    sink_rows = jnp.repeat(sinks_a[0].astype(F32), ntok).reshape(A_HEADS * ntok, 1)
    slope_rows = jnp.repeat(a_slopes, ntok).reshape(A_HEADS * ntok, 1)
    r16, wks, wvs = _win_sample(qrows, kbuf, knew, vbuf, vnew, sink_rows, slope_rows)
    o = r16.reshape(nseq, A_GROUP, ntok, A_KV_HEADS, HEAD_DIM)
    o = jnp.transpose(o, (2, 0, 3, 1, 4)).reshape(ntok * nseq, D_MODEL).astype(BF16)
    xs = _out_ffn(o, xs, mod_s, g, wo, wgu, wd, layer, SAMPLE_TILE, sample_tiles)
    outs["wks"] = wks.reshape(1, nseq, WINDOW, A_KV_HEADS, HEAD_DIM)
    outs["wvs"] = wvs.reshape(1, nseq, WINDOW, A_KV_HEADS, HEAD_DIM)

    layer = 1
    lam_init = 0.8 - 0.6 * math.exp(-0.3 * layer)
    mod_p = mods[layer, :batch].reshape(batch, 1, N_MOD * d)
    mod_s = mods[layer, batch:].reshape(1, nseq, N_MOD * d)
    g = norm_g[layer]
    wqkv = w_qkv_b[0].astype(BF16)
    wo = w_o_b[0].astype(BF16)
    lam_p = lambda_b[0].astype(F32)
    subln = subln_b[0].astype(F32)
    b_slopes = jnp.asarray([_b_slope(h) for h in range(B_HEADS)], F32)

    wkv = wqkv[:, D_MODEL:]
    wqt = wqkv[:, :D_MODEL].T
    wvt = wqkv[:, D_MODEL + b_nk:].T
    k, v, kb, qt, vt = _qkv_t(xp, mod_p, g, wkv, wqt, wvt, D_MODEL, b_nk, ROW_TILE, prompt_tiles)
    o = _diff_prompt(qt, kb, vt, b_slopes, lam_p, subln.reshape(B_DK, 1), batch, seq, lam_init)
    xp = _out_ffn(o, xp, mod_p, g, wo, wgu, wd, layer, ROW_TILE, prompt_tiles)
    outs["dkp"] = k.reshape(1, batch, seq, B_KV_HEADS, B_DK)
    outs["dvp"] = v.reshape(1, batch, seq, B_KV_HEADS, B_DK)

    q, k, v = _qkv(xs, mod_s, g, wqkv, D_MODEL, b_nk, SAMPLE_TILE, sample_tiles)
    q6 = q.reshape(ntok, nseq, B_KV_HEADS, B_GROUP, 2, HEAD_DIM)
    q6 = jnp.transpose(q6, (1, 4, 2, 3, 0, 5))
    eye_m = jnp.eye(2, dtype=BF16)
    qrows = (q6[:, :, :, :, :, None, :] * eye_m[None, :, None, None, None, :, None]).reshape(
        nseq, 2 * B_HEADS * ntok, B_DK)
    ks = _to_seq_major(k, nseq)
    vs = _to_seq_major(v, nseq)
    bias, nbias, slope_rows, t_rows = _diff_sample_bias(b_slopes)
    ck = cache_diff_k.reshape(n_pool, PAGE_ROWS, B_DK)
    cv = cache_diff_v.reshape(n_pool, PAGE_ROWS, B_DK)
    r32 = _diff_sample(page_table, qrows, ks.reshape(nseq, NEW_ROWS, B_DK), vs.reshape(nseq, NEW_ROWS, B_DK),
                       ck, cv, bias, nbias, slope_rows, t_rows, lam_p, subln.reshape(1, B_DK), lam_init)
    o = r32.reshape(nseq, B_KV_HEADS, B_GROUP, ntok, B_DK)
    o = jnp.transpose(o, (3, 0, 1, 2, 4)).reshape(ntok * nseq, D_MODEL).astype(BF16)
    xs = _out_ffn(o, xs, mod_s, g, wo, wgu, wd, layer, SAMPLE_TILE, sample_tiles)
    outs["dks"] = ks.reshape(1, nseq, ntok, B_KV_HEADS, B_DK)
    outs["dvs"] = vs.reshape(1, nseq, ntok, B_KV_HEADS, B_DK)

    y_prompt = xp.reshape(batch, seq, d)
    y_sample = _to_seq_major(xs, nseq)
    return (y_prompt, y_sample, outs["wkp"], outs["wvp"], outs["wks"], outs["wvs"],
            outs["dkp"], outs["dvp"], outs["dks"], outs["dvs"])
```
